```python
import jax, jax.numpy as jnp
from jax import lax
import numpy as np

D_MODEL = 1024
BATCH = 16
SEQ = 2048
DEPTH = 4

N_META = 16
HEAD_DIM = 64
A_Q_HEADS = D_MODEL // HEAD_DIM
A_KV_HEADS = A_Q_HEADS // 4
IDX_HEADS = 8
IDX_DIM = 64
B_HEADS = D_MODEL // HEAD_DIM
D_FF = 4 * D_MODEL
BLOCK = 128
TOPK_MAX = 256
ROPE_THETA = 10000.0
EPS = 1e-6
N_A = DEPTH // 2
N_B = DEPTH - N_A

A_Q_W = A_Q_HEADS * HEAD_DIM
A_KV_W = A_KV_HEADS * HEAD_DIM
IDX_Q_W = IDX_HEADS * IDX_DIM
A_SPLITS = (A_Q_W, A_Q_W + A_KV_W, A_Q_W + 2 * A_KV_W,
            A_Q_W + 2 * A_KV_W + IDX_Q_W, A_Q_W + 2 * A_KV_W + IDX_Q_W + IDX_DIM)
A_IN_W = A_SPLITS[-1] + IDX_HEADS
B_W = B_HEADS * HEAD_DIM
KV_IN_W = 2 * B_W + B_HEADS

kernel_name = "yoco_dsa_fox_hybrid"


def rms_norm(x, g):
    xf = x.astype(jnp.float32)
    y = xf * lax.rsqrt(jnp.mean(xf * xf, axis=-1, keepdims=True) + EPS)
    return (y * g.astype(jnp.float32)).astype(x.dtype)


def rope_tables(T, dim):
    inv = 1.0 / (ROPE_THETA ** (jnp.arange(0, dim, 2, dtype=jnp.float32) / dim))
    ang = jnp.arange(T, dtype=jnp.float32)[:, None] * inv[None, :]
    return jnp.cos(ang), jnp.sin(ang)


def rope(x, cos, sin):
    x1, x2 = jnp.split(x, 2, axis=-1)
    c = cos[None, :, None, :]
    s = sin[None, :, None, :]
    return jnp.concatenate([x1 * c - x2 * s, x1 * s + x2 * c], axis=-1).astype(x.dtype)


def sq_relu_mlp(x, w1, w2):
    h = jax.nn.relu(x @ w1)
    return (h * h) @ w2


def dsa_attention(h, w_in, q_gain, k_gain, w_out, cos, sin, topk):
    B, T, _ = h.shape
    R = A_Q_HEADS // A_KV_HEADS
    nblk = T // BLOCK
    q, k, v, qi, ki, wi = jnp.split(h @ w_in, list(A_SPLITS), axis=-1)
    q = rope(rms_norm(q.reshape(B, T, A_Q_HEADS, HEAD_DIM), q_gain), cos, sin)
    k = rope(rms_norm(k.reshape(B, T, A_KV_HEADS, HEAD_DIM), k_gain), cos, sin)
    v = v.reshape(B, T, A_KV_HEADS, HEAD_DIM)
    qi = rope(qi.reshape(B, T, IDX_HEADS, IDX_DIM), cos, sin)
    ki = rope(ki[:, :, None, :], cos, sin)[:, :, 0, :]
    wi = wi.astype(jnp.float32) * (IDX_HEADS ** -0.5 * IDX_DIM ** -0.5)
    qb = q.reshape(B, nblk, BLOCK, A_KV_HEADS, R, HEAD_DIM)
    qib = qi.reshape(B, nblk, BLOCK, IDX_HEADS, IDX_DIM)
    wib = wi.reshape(B, nblk, BLOCK, IDX_HEADS)
    key_pos = jnp.arange(T)
    scale = HEAD_DIM ** -0.5

    def per_seq(args):
        qb_s, qib_s, wib_s, k_s, v_s, ki_s = args

        def per_block(bargs):
            blk, q_blk, qi_blk, w_blk = bargs
            qpos = blk * BLOCK + jnp.arange(BLOCK)
            causal = key_pos[None, :] <= qpos[:, None]
            logits = jnp.einsum('qhd,sd->qhs', qi_blk, ki_s,
                                preferred_element_type=jnp.float32)
            score = jnp.einsum('qh,qhs->qs', w_blk, jax.nn.relu(logits))
            score = jnp.where(causal, score, -jnp.inf)
            _, idx = lax.top_k(score, topk)
            valid = idx <= qpos[:, None]
            k_sel = k_s[idx]
            v_sel = v_s[idx]
            s = jnp.einsum('qgrd,qkgd->qgrk', q_blk, k_sel,
                           preferred_element_type=jnp.float32) * scale
            s = jnp.where(valid[:, None, None, :], s, -jnp.inf)
            p = jax.nn.softmax(s, axis=-1).astype(v_s.dtype)
            return jnp.einsum('qgrk,qkgd->qgrd', p, v_sel)

        return lax.map(per_block, (jnp.arange(nblk), qb_s, qib_s, wib_s))

    o = lax.map(per_seq, (qb, qib, wib, k, v, ki))
    return o.reshape(B, T, A_Q_W) @ w_out


def shared_kv(h, w_kv, f_bias, k_gain):
    B, T, _ = h.shape
    k, v, fl = jnp.split(h @ w_kv, [B_W, 2 * B_W], axis=-1)
    k = rms_norm(k.reshape(B, T, B_HEADS, HEAD_DIM), k_gain)
    v = v.reshape(B, T, B_HEADS, HEAD_DIM)
    log_f = jax.nn.log_sigmoid(fl.astype(jnp.float32) + f_bias.astype(jnp.float32))
    c = jnp.cumsum(log_f, axis=1).transpose(0, 2, 1)
    return k, v, c


def fox_attention(h, w_q, q_gain, w_out, k, v, c):
    B, T, _ = h.shape
    q = rms_norm((h @ w_q).reshape(B, T, B_HEADS, HEAD_DIM), q_gain)
    scale = HEAD_DIM ** -0.5
    outs = []
    for blk in range(T // BLOCK):
        q0, q1 = blk * BLOCK, (blk + 1) * BLOCK
        s = jnp.einsum('bqhd,bshd->bhqs', q[:, q0:q1], k[:, :q1],
                       preferred_element_type=jnp.float32) * scale
        decay = c[:, :, q0:q1, None] - c[:, :, None, :q1]
        causal = jnp.arange(q0, q1)[:, None] >= jnp.arange(q1)[None, :]
        s = jnp.where(causal, s + decay, -jnp.inf)
        p = jax.nn.softmax(s, axis=-1).astype(v.dtype)
        outs.append(jnp.einsum('bhqs,bshd->bqhd', p, v[:, :q1]))
    o = jnp.concatenate(outs, axis=1)
    return o.reshape(B, T, B_W) @ w_out


def setup_inputs(seed: int = 0) -> dict:
    key = jax.random.key(seed)
    ks = jax.random.split(key, 18)

    def dense(k, shape, fan_in, scale=1.0):
        return jax.random.normal(k, shape, jnp.float32) * (scale * fan_in ** -0.5)

    def gain(k, shape):
        return 1.0 + 0.02 * jax.random.normal(k, shape, jnp.float32)

    kv_kv = dense(ks[11], (D_MODEL, 2 * B_W), D_MODEL)
    kv_f = dense(ks[17], (D_MODEL, B_HEADS), D_MODEL, 0.25)
    return {
        "x": jax.random.normal(ks[0], (BATCH, SEQ, D_MODEL), jnp.float32),
        "meta_tokens": jax.random.normal(ks[1], (N_META, D_MODEL), jnp.float32),
        "attn_norm": gain(ks[2], (DEPTH, D_MODEL)),
        "mlp_norm": gain(ks[3], (DEPTH, D_MODEL)),
        "mlp_w1": dense(ks[4], (DEPTH, D_MODEL, D_FF), D_MODEL),
        "mlp_w2": dense(ks[5], (DEPTH, D_FF, D_MODEL), D_FF),
        "a_w_in": dense(ks[6], (N_A, D_MODEL, A_IN_W), D_MODEL),
        "a_q_gain": gain(ks[7], (N_A, HEAD_DIM)),
        "a_k_gain": gain(ks[8], (N_A, HEAD_DIM)),
        "a_w_out": dense(ks[9], (N_A, A_Q_W, D_MODEL), A_Q_W),
        "kv_norm": gain(ks[10], (D_MODEL,)),
        "kv_w": jnp.concatenate([kv_kv, kv_f], axis=1),
        "kv_f_bias": jax.random.uniform(ks[12], (B_HEADS,), jnp.float32, 1.0, 4.0),
        "kv_k_gain": gain(ks[13], (HEAD_DIM,)),
        "b_w_q": dense(ks[14], (N_B, D_MODEL, B_W), D_MODEL),
        "b_q_gain": gain(ks[15], (N_B, HEAD_DIM)),
        "b_w_out": dense(ks[16], (N_B, B_W, D_MODEL), B_W),
    }


def reference(x, meta_tokens, attn_norm, mlp_norm, mlp_w1, mlp_w2,
              a_w_in, a_q_gain, a_k_gain, a_w_out,
              kv_norm, kv_w, kv_f_bias, kv_k_gain,
              b_w_q, b_q_gain, b_w_out):
    B, S, D = x.shape
    T = S + N_META
    Tp = -(-T // BLOCK) * BLOCK
    meta = jnp.broadcast_to(meta_tokens.astype(x.dtype)[None], (B, N_META, D))
    h = jnp.concatenate([meta, x], axis=1)
    h = jnp.pad(h, ((0, 0), (0, Tp - T), (0, 0)))
    cos, sin = rope_tables(Tp, HEAD_DIM)
    topk = min(TOPK_MAX, S // 4)
    shared = None
    for i in range(DEPTH):
        hn = rms_norm(h, attn_norm[i])
        if i < N_A:
            h = h + dsa_attention(hn, a_w_in[i], a_q_gain[i], a_k_gain[i], a_w_out[i],
                                  cos, sin, topk)
        else:
            if i == N_A:
                shared = shared_kv(rms_norm(h, kv_norm), kv_w, kv_f_bias, kv_k_gain)
            j = i - N_A
            h = h + fox_attention(hn, b_w_q[j], b_q_gain[j], b_w_out[j],
                                  shared[0], shared[1], shared[2])
        h = h + sq_relu_mlp(rms_norm(h, mlp_norm[i]), mlp_w1[i], mlp_w2[i])
    return h[:, N_META:N_META + S]
```

```python
import functools
import math

import jax
import jax.numpy as jnp
from jax import lax
from jax.experimental import pallas as pl
from jax.experimental.pallas import tpu as pltpu

F32 = jnp.float32
BF16 = jnp.bfloat16

D_MODEL = 1024
HEAD_DIM = 64
HALF = HEAD_DIM // 2
N_META = 16
BLOCK = 128
CHUNK = 256
A_Q_HEADS = 16
A_KV_HEADS = 4
A_REP = A_Q_HEADS // A_KV_HEADS
IDX_HEADS = 8
B_HEADS = 16
D_FF = 4 * D_MODEL
FF_CHUNK = 1024
TOPK = 256
EPS = 1e-6
ROPE_THETA = 10000.0
LOG2E = math.log2(math.e)
NEG = -1e30
N_BISECT = 20
IDX_BITS = 12
LANES = 128
VMEM_LIMIT = 52 * 1024 * 1024


def _cparams(*sem):
    return pltpu.CompilerParams(dimension_semantics=sem, vmem_limit_bytes=VMEM_LIMIT)


def _dot(a, b):
    return jnp.dot(a, b, preferred_element_type=F32)


def _split2(x):
    hi = x.astype(BF16)
    lo = (x - hi.astype(F32)).astype(BF16)
    return hi, lo


def _split3(x):
    hi = x.astype(BF16)
    r = x - hi.astype(F32)
    mid = r.astype(BF16)
    lo = (r - mid.astype(F32)).astype(BF16)
    return hi, mid, lo


def _rms_rows(h, g):
    ms = jnp.mean(h * h, axis=-1, keepdims=True)
    return h * lax.rsqrt(ms + EPS) * g


def _head_norm_nat(x, e_ref, et_ref, gain):
    x2 = x * x
    hi, lo = _split2(x2)
    ss = _dot(hi, e_ref[...]) + _dot(lo, e_ref[...])
    inv = lax.rsqrt(ss * (1.0 / HEAD_DIM) + EPS)
    ihi, ilo = _split2(inv)
    invx = _dot(ihi, et_ref[...]) + _dot(ilo, et_ref[...])
    return x * invx * gain


def _rope_nat(x, c2, s2):
    w = x.shape[1]
    reps = w // LANES
    c = c2 if reps == 1 else jnp.concatenate([c2] * reps, axis=1)
    s = s2 if reps == 1 else jnp.concatenate([s2] * reps, axis=1)
    lane = lax.broadcasted_iota(jnp.int32, x.shape, 1)
    first = (lane & HALF) == 0
    partner = jnp.where(first, pltpu.roll(x, w - HALF, 1), pltpu.roll(x, HALF, 1))
    return x * c + partner * s


def _proj_a_kernel(h_ref, g_ref, w_ref, e_ref, et_ref, kg_ref, c2_ref, s2_ref,
                   q_o, k_o, v_o, qi_o, ki_o, wi_o):
    hn = _rms_rows(h_ref[0], g_ref[...]).astype(BF16)
    y = _dot(hn, w_ref[...])
    q_o[0] = y[:, 0:1024].astype(BF16)
    c2 = c2_ref[...]
    s2 = s2_ref[...]
    k = _head_norm_nat(y[:, 1024:1280], e_ref, et_ref, kg_ref[...])
    k_o[0] = _rope_nat(k, c2, s2).astype(BF16)
    v_o[0] = y[:, 1280:1536].astype(BF16)
    qi_o[0] = y[:, 1536:2048].astype(BF16)
    kiw = y[:, 2048:2176]
    ki_o[0] = _rope_nat(kiw, c2, s2)[:, 0:HEAD_DIM].astype(BF16)
    wi_o[0] = kiw * (IDX_HEADS ** -0.5 * HEAD_DIM ** -0.5)


def _proj_a(h, g, w_all, e, et, kgain, c2, s2, tm):
    b, tp, d = h.shape
    nt = tp // tm
    wn = w_all.shape[1]
    const = lambda *_: (0, 0)
    tile = lambda bi, j: (bi, j, 0)
    return pl.pallas_call(
        _proj_a_kernel,
        grid=(b, nt),
        in_specs=[
            pl.BlockSpec((1, tm, d), tile),
            pl.BlockSpec((1, d), const),
            pl.BlockSpec((d, wn), const),
            pl.BlockSpec(e.shape, const),
            pl.BlockSpec(et.shape, const),
            pl.BlockSpec((1, 256), const),
            pl.BlockSpec((tm, LANES), lambda bi, j: (j, 0)),
            pl.BlockSpec((tm, LANES), lambda bi, j: (j, 0)),
        ],
        out_specs=[
            pl.BlockSpec((1, tm, 1024), tile),
            pl.BlockSpec((1, tm, 256), tile),
            pl.BlockSpec((1, tm, 256), tile),
            pl.BlockSpec((1, tm, 512), tile),
            pl.BlockSpec((1, tm, HEAD_DIM), tile),
            pl.BlockSpec((1, tm, LANES), tile),
        ],
        out_shape=[
            jax.ShapeDtypeStruct((b, tp, 1024), BF16),
            jax.ShapeDtypeStruct((b, tp, 256), BF16),
            jax.ShapeDtypeStruct((b, tp, 256), BF16),
            jax.ShapeDtypeStruct((b, tp, 512), BF16),
            jax.ShapeDtypeStruct((b, tp, HEAD_DIM), BF16),
            jax.ShapeDtypeStruct((b, tp, LANES), F32),
        ],
        compiler_params=_cparams("parallel", "parallel"),
        name="proj_a",
    )(h, g, w_all, e, et, kgain, c2, s2)


def _head_T(xT, h, cosT, sinT, gain, scale):
    x = xT[HEAD_DIM * h:HEAD_DIM * (h + 1)]
    if gain is not None:
        ss = jnp.sum(x * x, axis=0, keepdims=True)
        x = x * lax.rsqrt(ss * (1.0 / HEAD_DIM) + EPS) * gain
    if cosT is not None:
        x1 = x[:HALF]
        x2 = x[HALF:]
        x = jnp.concatenate([x1 * cosT - x2 * sinT, x1 * sinT + x2 * cosT], axis=0)
    if scale is not None:
        x = x * scale
    return x


def _softmax_step(s, h, m_ref, l_ref):
    m_old = m_ref[h:h + 1, :]
    m_new = jnp.maximum(m_old, jnp.max(s, axis=0, keepdims=True))
    alpha = jnp.exp2(m_old - m_new)
    p = jnp.exp2(s - m_new)
    l_ref[h:h + 1, :] = alpha * l_ref[h:h + 1, :] + jnp.sum(p, axis=0, keepdims=True)
    m_ref[h:h + 1, :] = m_new
    return p, alpha


def _finish_heads(nheads, l_ref, acc_ref, o_ref):
    outs = [acc_ref[h] / l_ref[h:h + 1, :] for h in range(nheads)]
    oT = jnp.concatenate(outs, axis=0)
    o_ref[0] = oT.T.astype(BF16)


def _dsa_kernel(q_ref, k_ref, v_ref, qi_ref, ki_ref, wi_ref, cos_ref, sin_ref, qg_ref,
                o_ref, sc_ref, qx_ref, qix_ref, thr_ref, cut_ref, m_ref, l_ref, acc_ref):
    blk = pl.program_id(1)
    nch = (blk + 2) // 2
    cosT = cos_ref[...]
    sinT = sin_ref[...]

    qT = q_ref[0].astype(F32).T
    qscale = HEAD_DIM ** -0.5 * LOG2E
    for g in range(A_KV_HEADS):
        heads = [_head_T(qT, A_REP * g + j, cosT, sinT, qg_ref[...], qscale).astype(BF16)
                 for j in range(A_REP)]
        row = jnp.concatenate(heads, axis=1)
        parts = []
        if g > 0:
            parts.append(jnp.zeros((HEAD_DIM * g, A_REP * BLOCK), BF16))
        parts.append(row)
        if g < A_KV_HEADS - 1:
            parts.append(jnp.zeros((HEAD_DIM * (A_KV_HEADS - 1 - g), A_REP * BLOCK), BF16))
        qx_ref[g] = jnp.concatenate(parts, axis=0)

    qiT = qi_ref[0].astype(F32).T
    qix_ref[...] = jnp.concatenate(
        [_head_T(qiT, h, cosT, sinT, None, None).astype(BF16) for h in range(IDX_HEADS)], axis=1)
    wT = wi_ref[0].T

    qidx = blk * BLOCK + lax.broadcasted_iota(jnp.int32, (CHUNK, BLOCK), 1)
    krow = lax.broadcasted_iota(jnp.int32, (CHUNK, BLOCK), 0)

    def idx_body(c, carry):
        off = pl.multiple_of(c * CHUNK, CHUNK)
        lg = _dot(ki_ref[0, pl.ds(off, CHUNK), :], qix_ref[...])
        acc = jnp.zeros((CHUNK, BLOCK), F32)
        for h in range(IDX_HEADS):
            w_h = wT[HEAD_DIM + h:HEAD_DIM + h + 1, :]
            acc = acc + w_h * jnp.maximum(lg[:, BLOCK * h:BLOCK * (h + 1)], 0.0)
        sc_ref[pl.ds(off, CHUNK), :] = jnp.where(krow + off <= qidx, acc, -jnp.inf)
        return carry

    lax.fori_loop(0, nch, idx_body, 0)

    thr_ref[...] = jnp.full((1, BLOCK), -jnp.inf, F32)
    cut_ref[...] = jnp.full((1, BLOCK), -1, jnp.int32)

    def reduce_chunks(fn, init, combine):
        def body(c, acc):
            off = pl.multiple_of(c * CHUNK, CHUNK)
            val = fn(sc_ref[pl.ds(off, CHUNK), :], off)
            part = val.reshape(CHUNK // 8, 8, BLOCK)
            return combine(acc, part)
        return lax.fori_loop(0, nch, body, init)

    def count(pred_fn):
        acc = reduce_chunks(lambda s, off: jnp.where(pred_fn(s, off), 1.0, 0.0),
                            jnp.zeros((8, BLOCK), F32),
                            lambda a, p: a + jnp.sum(p, axis=0))
        return jnp.sum(acc, axis=0, keepdims=True)

    def masked_max(pred_fn):
        acc = reduce_chunks(lambda s, off: jnp.where(pred_fn(s, off), s, -jnp.inf),
                            jnp.full((8, BLOCK), -jnp.inf, F32),
                            lambda a, p: jnp.maximum(a, jnp.max(p, axis=0)))
        return jnp.max(acc, axis=0, keepdims=True)

    def masked_min(pred_fn):
        acc = reduce_chunks(lambda s, off: jnp.where(pred_fn(s, off), s, jnp.inf),
                            jnp.full((8, BLOCK), jnp.inf, F32),
                            lambda a, p: jnp.minimum(a, jnp.min(p, axis=0)))
        return jnp.min(acc, axis=0, keepdims=True)

    kf = float(TOPK)

    @pl.when(blk >= 2)
    def _():
        causal = lambda s, off: s > -jnp.inf
        vmax = masked_max(causal)
        vmin = masked_min(causal)
        top_full = count(lambda s, off: s >= vmax) >= kf

        def bis(_, lohi):
            lo, hi = lohi
            mid = 0.5 * (lo + hi)
            ok = count(lambda s, off: s >= mid) >= kf
            return jnp.where(ok, mid, lo), jnp.where(ok, hi, mid)

        lo, hi = lax.fori_loop(0, N_BISECT, bis, (vmin, vmax))

        thr0 = masked_max(lambda s, off: s < hi)
        cnt0 = count(lambda s, off: s >= thr0)

        def walk_cond(st):
            thr, cnt = st
            return jnp.max(jnp.where(cnt < kf, 1.0, 0.0)) > 0.0

        def walk_body(st):
            thr, cnt = st
            nxt = masked_max(lambda s, off: s < thr)
            thr = jnp.where(cnt < kf, nxt, thr)
            return thr, count(lambda s, off: s >= thr)

        thr, cnt_ge = lax.while_loop(walk_cond, walk_body, (thr0, cnt0))
        thr = jnp.where(top_full, vmax, thr)
        cnt_ge = jnp.where(top_full, count(lambda s, off: s >= vmax), cnt_ge)
        thr_ref[...] = thr
        cut_ref[...] = jnp.full((1, BLOCK), 2 ** IDX_BITS, jnp.int32)

        @pl.when(jnp.max(cnt_ge) > kf)
        def _():
            need = kf - count(lambda s, off: s > thr)

            def tie_bis(i, lo_i):
                half = lax.shift_right_logical(jnp.int32(2 ** (IDX_BITS - 1)), i)
                mid = lo_i + half
                got = count(lambda s, off: (s == thr) & (krow + off <= mid))
                return jnp.where(got >= need, lo_i, mid)

            lo_i = lax.fori_loop(0, IDX_BITS, tie_bis, jnp.full((1, BLOCK), -1, jnp.int32))
            cut_ref[...] = lo_i + 1

    m_ref[...] = jnp.full(m_ref.shape, NEG, F32)
    l_ref[...] = jnp.zeros(l_ref.shape, F32)
    acc_ref[...] = jnp.zeros(acc_ref.shape, F32)

    def att_body(c, carry):
        off = pl.multiple_of(c * CHUNK, CHUNK)
        s_idx = sc_ref[pl.ds(off, CHUNK), :]
        thr = thr_ref[...]
        sel = (s_idx > thr) | ((s_idx == thr) & (krow + off <= cut_ref[...]))
        bias = jnp.where(sel, 0.0, NEG)
        kc = k_ref[0, pl.ds(off, CHUNK), :]
        vT = v_ref[0, pl.ds(off, CHUNK), :].astype(F32).T.astype(BF16)
        for g in range(A_KV_HEADS):
            sT = _dot(kc, qx_ref[g])
            ps, alphas = [], []
            for j in range(A_REP):
                p, alpha = _softmax_step(sT[:, BLOCK * j:BLOCK * (j + 1)] + bias, A_REP * g + j, m_ref, l_ref)
                ps.append(p.astype(BF16))
                alphas.append(alpha)
            pv = _dot(vT[HEAD_DIM * g:HEAD_DIM * (g + 1)], jnp.concatenate(ps, axis=1))
            for j in range(A_REP):
                h = A_REP * g + j
                acc_ref[h] = alphas[j] * acc_ref[h] + pv[:, BLOCK * j:BLOCK * (j + 1)]
        return carry

    lax.fori_loop(0, nch, att_body, 0)
    _finish_heads(A_Q_HEADS, l_ref, acc_ref, o_ref)


def _dsa_attention(q, k, v, qi, ki, wi, cosT, sinT, qgain):
    b, tp, _ = q.shape
    tk = k.shape[1]
    nblk = tp // BLOCK
    qblk = lambda bi, j: (bi, j, 0)
    seq = lambda bi, j: (bi, 0, 0)
    return pl.pallas_call(
        _dsa_kernel,
        grid=(b, nblk),
        in_specs=[
            pl.BlockSpec((1, BLOCK, 1024), qblk),
            pl.BlockSpec((1, tk, 256), seq),
            pl.BlockSpec((1, tk, 256), seq),
            pl.BlockSpec((1, BLOCK, 512), qblk),
            pl.BlockSpec((1, tk, HEAD_DIM), seq),
            pl.BlockSpec((1, BLOCK, LANES), qblk),
            pl.BlockSpec((HALF, BLOCK), lambda bi, j: (0, j)),
            pl.BlockSpec((HALF, BLOCK), lambda bi, j: (0, j)),
            pl.BlockSpec((HEAD_DIM, 1), lambda bi, j: (0, 0)),
        ],
        out_specs=pl.BlockSpec((1, BLOCK, 1024), qblk),
        out_shape=jax.ShapeDtypeStruct((b, tp, 1024), BF16),
        scratch_shapes=[
            pltpu.VMEM((tk, BLOCK), F32),
            pltpu.VMEM((A_KV_HEADS, 256, A_REP * BLOCK), BF16),
            pltpu.VMEM((HEAD_DIM, IDX_HEADS * BLOCK), BF16),
            pltpu.VMEM((1, BLOCK), F32),
            pltpu.VMEM((1, BLOCK), jnp.int32),
            pltpu.VMEM((A_Q_HEADS, BLOCK), F32),
            pltpu.VMEM((A_Q_HEADS, BLOCK), F32),
            pltpu.VMEM((A_Q_HEADS, HEAD_DIM, BLOCK), F32),
        ],
        compiler_params=_cparams("parallel", "arbitrary"),
        name="dsa_attn",
    )(q, k, v, qi, ki, wi, cosT, sinT, qgain)


def _mlp_kernel(h_ref, o_ref, wo_ref, g_ref, w1_ref, w2_ref, out_ref):
    h1 = h_ref[...] + _dot(o_ref[...], wo_ref[...])
    hn = _rms_rows(h1, g_ref[...]).astype(BF16)
    acc = h1
    for c in range(D_FF // FF_CHUNK):
        u = jnp.maximum(_dot(hn, w1_ref[:, FF_CHUNK * c:FF_CHUNK * (c + 1)]), 0.0)
        acc = acc + _dot((u * u).astype(BF16), w2_ref[FF_CHUNK * c:FF_CHUNK * (c + 1), :])
    out_ref[...] = acc


def _out_mlp(h2, o2, wo, g, w1, w2, tm):
    n, d = h2.shape
    const = lambda i: (0, 0)
    tile = lambda i: (i, 0)
    once = pl.Buffered(1)
    return pl.pallas_call(
        _mlp_kernel,
        grid=(n // tm,),
        in_specs=[
            pl.BlockSpec((tm, d), tile),
            pl.BlockSpec((tm, d), tile),
            pl.BlockSpec((d, d), const, pipeline_mode=once),
            pl.BlockSpec((1, d), const),
            pl.BlockSpec((d, D_FF), const, pipeline_mode=once),
            pl.BlockSpec((D_FF, d), const, pipeline_mode=once),
        ],
        out_specs=pl.BlockSpec((tm, d), tile),
        out_shape=jax.ShapeDtypeStruct((n, d), F32),
        compiler_params=_cparams("parallel"),
        name="out_mlp",
    )(h2, o2, wo, g, w1, w2)


def _shared_kv_kernel(h_ref, g_ref, w_ref, e_ref, et_ref, kg_ref, fb_ref, k_o, v_o, lf_o):
    hn = _rms_rows(h_ref[0], g_ref[...]).astype(BF16)
    y = _dot(hn, w_ref[...])
    k_o[0] = _head_norm_nat(y[:, 0:1024], e_ref, et_ref, kg_ref[...]).astype(BF16)
    v_o[0] = y[:, 1024:2048].astype(BF16)
    x = y[:, 2048:2176] + fb_ref[...]
    lf_o[0] = jnp.minimum(x, 0.0) - jnp.log1p(jnp.exp(-jnp.abs(x)))


def _shared_kv(h, g, w_all, e, et, kgain, fbias, tm):
    b, tp, d = h.shape
    const = lambda *_: (0, 0)
    tile = lambda bi, j: (bi, j, 0)
    return pl.pallas_call(
        _shared_kv_kernel,
        grid=(b, tp // tm),
        in_specs=[
            pl.BlockSpec((1, tm, d), tile),
            pl.BlockSpec((1, d), const),
            pl.BlockSpec(w_all.shape, const),
            pl.BlockSpec(e.shape, const),
            pl.BlockSpec(et.shape, const),
            pl.BlockSpec((1, 1024), const),
            pl.BlockSpec((1, LANES), const),
        ],
        out_specs=[
            pl.BlockSpec((1, tm, 1024), tile),
            pl.BlockSpec((1, tm, 1024), tile),
            pl.BlockSpec((1, tm, LANES), tile),
        ],
        out_shape=[
            jax.ShapeDtypeStruct((b, tp, 1024), BF16),
            jax.ShapeDtypeStruct((b, tp, 1024), BF16),
            jax.ShapeDtypeStruct((b, tp, LANES), F32),
        ],
        compiler_params=_cparams("parallel", "parallel"),
        name="shared_kv",
    )(h, g, w_all, e, et, kgain, fbias)


def _cumsum_kernel(lf_ref, c_ref):
    r = lax.broadcasted_iota(jnp.int32, (BLOCK, BLOCK), 0)
    cidx = lax.broadcasted_iota(jnp.int32, (BLOCK, BLOCK), 1)
    tri = jnp.where(cidx <= r, 1.0, 0.0).astype(BF16)
    ntile = lf_ref.shape[1] // BLOCK

    def body(i, carry):
        off = pl.multiple_of(i * BLOCK, BLOCK)
        hi, mid, lo = _split3(lf_ref[0, pl.ds(off, BLOCK), :])
        cs = _dot(tri, hi) + _dot(tri, mid) + _dot(tri, lo) + carry
        c_ref[0, pl.ds(off, BLOCK), :] = cs
        return cs[BLOCK - 1:BLOCK, :]

    lax.fori_loop(0, ntile, body, jnp.zeros((1, LANES), F32))


def _cumsum_t(lf):
    b, tp, w = lf.shape
    spec = pl.BlockSpec((1, tp, w), lambda bi: (bi, 0, 0))
    return pl.pallas_call(
        _cumsum_kernel,
        grid=(b,),
        in_specs=[spec],
        out_specs=spec,
        out_shape=jax.ShapeDtypeStruct((b, tp, w), F32),
        compiler_params=_cparams("parallel"),
        name="cumsum_t",
    )(lf)


def _proj_q_kernel(h_ref, g_ref, w_ref, q_o):
    hn = _rms_rows(h_ref[...], g_ref[...]).astype(BF16)
    q_o[...] = _dot(hn, w_ref[...]).astype(BF16)


def _proj_q(h2, g, w, tm):
    n, d = h2.shape
    const = lambda i: (0, 0)
    tile = lambda i: (i, 0)
    return pl.pallas_call(
        _proj_q_kernel,
        grid=(n // tm,),
        in_specs=[pl.BlockSpec((tm, d), tile), pl.BlockSpec((1, d), const), pl.BlockSpec((d, d), const)],
        out_specs=pl.BlockSpec((tm, d), tile),
        out_shape=jax.ShapeDtypeStruct((n, d), BF16),
        compiler_params=_cparams("parallel"),
        name="proj_q",
    )(h2, g, w)


def _fox_kernel(q_ref, k_ref, v_ref, c_ref, qg_ref, o_ref, qx_ref, m_ref, l_ref, acc_ref):
    blk = pl.program_id(1)
    nch = (blk + 2) // 2
    qT = q_ref[0].astype(F32).T
    qscale = HEAD_DIM ** -0.5 * LOG2E
    for h in range(B_HEADS):
        x = _head_T(qT, h, None, None, qg_ref[...], qscale).astype(BF16)
        j = h % 4
        parts = []
        if j > 0:
            parts.append(jnp.zeros((HEAD_DIM * j, BLOCK), BF16))
        parts.append(x)
        if j < 3:
            parts.append(jnp.zeros((HEAD_DIM * (3 - j), BLOCK), BF16))
        qx_ref[h] = jnp.concatenate(parts, axis=0)

    m_ref[...] = jnp.full(m_ref.shape, NEG, F32)
    l_ref[...] = jnp.zeros(l_ref.shape, F32)
    acc_ref[...] = jnp.zeros(acc_ref.shape, F32)
    qidx = blk * BLOCK + lax.broadcasted_iota(jnp.int32, (CHUNK, BLOCK), 1)
    krow = lax.broadcasted_iota(jnp.int32, (CHUNK, BLOCK), 0)

    def att_body(c, carry):
        off = pl.multiple_of(c * CHUNK, CHUNK)
        bias = jnp.where(krow + off <= qidx, 0.0, NEG)
        cl = c_ref[0, pl.ds(off, CHUNK), :] * LOG2E
        for h in range(B_HEADS):
            col = 256 * (h // 4)
            kc = k_ref[0, pl.ds(off, CHUNK), col:col + 256]
            s = _dot(kc, qx_ref[h]) + (bias - cl[:, h:h + 1])
            p, alpha = _softmax_step(s, h, m_ref, l_ref)
            if h % 4 == 0:
                vT = v_ref[0, pl.ds(off, CHUNK), col:col + 256].astype(F32).T.astype(BF16)
            r0 = HEAD_DIM * (h % 4)
            pv = _dot(vT[r0:r0 + HEAD_DIM], p.astype(BF16))
            acc_ref[h] = alpha * acc_ref[h] + pv
        return carry

    lax.fori_loop(0, nch, att_body, 0)
    _finish_heads(B_HEADS, l_ref, acc_ref, o_ref)


def _fox_attention(q, k, v, c, qgain):
    b, tp, _ = q.shape
    tk = k.shape[1]
    qblk = lambda bi, j: (bi, j, 0)
    seq = lambda bi, j: (bi, 0, 0)
    return pl.pallas_call(
        _fox_kernel,
        grid=(b, tp // BLOCK),
        in_specs=[
            pl.BlockSpec((1, BLOCK, 1024), qblk),
            pl.BlockSpec((1, tk, 1024), seq),
            pl.BlockSpec((1, tk, 1024), seq),
            pl.BlockSpec((1, tk, LANES), seq),
            pl.BlockSpec((HEAD_DIM, 1), lambda bi, j: (0, 0)),
        ],
        out_specs=pl.BlockSpec((1, BLOCK, 1024), qblk),
        out_shape=jax.ShapeDtypeStruct((b, tp, 1024), BF16),
        scratch_shapes=[
            pltpu.VMEM((B_HEADS, 256, BLOCK), BF16),
            pltpu.VMEM((B_HEADS, BLOCK), F32),
            pltpu.VMEM((B_HEADS, BLOCK), F32),
            pltpu.VMEM((B_HEADS, HEAD_DIM, BLOCK), F32),
        ],
        compiler_params=_cparams("parallel", "arbitrary"),
        name="fox_attn",
    )(q, k, v, c, qgain)


def _indicator(nheads, width):
    e = (jnp.arange(nheads * HEAD_DIM)[:, None] // HEAD_DIM == jnp.arange(width)[None, :])
    return e.astype(BF16), e.T.astype(BF16)


def _pad_cols(w, n):
    return jnp.pad(w, ((0, 0), (0, n - w.shape[1])))


def _pad_time(x, tk):
    return jnp.pad(x, ((0, 0), (0, tk - x.shape[1]), (0, 0)))


def kernel(x, meta_tokens, attn_norm, mlp_norm, mlp_w1, mlp_w2, a_w_in, a_q_gain, a_k_gain, a_w_out,
           kv_norm, kv_w, kv_f_bias, kv_k_gain, b_w_q, b_q_gain, b_w_out):
    b, s, d = x.shape
    t = s + N_META
    tp = -(-t // BLOCK) * BLOCK
    tk = -(-tp // CHUNK) * CHUNK
    tm = tp // 4
    depth = attn_norm.shape[0]
    n_a = a_w_in.shape[0]
    assert d == D_MODEL and tm % 16 == 0 and min(TOPK, s // 4) == TOPK and tk <= 2 ** IDX_BITS

    meta = jnp.broadcast_to(meta_tokens.astype(x.dtype)[None], (b, N_META, d))
    h = jnp.pad(jnp.concatenate([meta, x], axis=1), ((0, 0), (0, tp - t), (0, 0)))

    inv = 1.0 / (ROPE_THETA ** (jnp.arange(0, HEAD_DIM, 2, dtype=F32) / HEAD_DIM))
    ang = jnp.arange(tp, dtype=F32)[:, None] * inv[None, :]
    cos, sin = jnp.cos(ang), jnp.sin(ang)
    c2 = jnp.concatenate([cos, cos, cos, cos], axis=1)
    s2 = jnp.concatenate([-sin, sin, -sin, sin], axis=1)
    cosT, sinT = cos.T, sin.T

    e4, et4 = _indicator(A_KV_HEADS, LANES)
    e16, et16 = _indicator(B_HEADS, LANES)

    shared = None
    for i in range(depth):
        g_attn = attn_norm[i][None, :]
        if i < n_a:
            w_all = _pad_cols(a_w_in[i], 2176).astype(BF16)
            kgain = jnp.tile(a_k_gain[i], A_KV_HEADS)[None, :]
            q, k, v, qi, ki, wi = _proj_a(h, g_attn, w_all, e4, et4, kgain, c2, s2, tm)
            o = _dsa_attention(q, _pad_time(k, tk), _pad_time(v, tk), qi, _pad_time(ki, tk), wi,
                               cosT, sinT, a_q_gain[i][:, None])
            wo = a_w_out[i]
        else:
            if shared is None:
                w_all = _pad_cols(kv_w, 2176).astype(BF16)
                kgain = jnp.tile(kv_k_gain, B_HEADS)[None, :]
                fbias = jnp.pad(kv_f_bias, (0, LANES - B_HEADS))[None, :]
                k, v, lf = _shared_kv(h, kv_norm[None, :], w_all, e16, et16, kgain, fbias, tm)
                shared = (_pad_time(k, tk), _pad_time(v, tk), _pad_time(_cumsum_t(lf), tk))
            j = i - n_a
            q = _proj_q(h.reshape(b * tp, d), g_attn, b_w_q[j].astype(BF16), tm).reshape(b, tp, d)
            o = _fox_attention(q, shared[0], shared[1], shared[2], b_q_gain[j][:, None])
            wo = b_w_out[j]
        h = _out_mlp(h.reshape(b * tp, d), o.reshape(b * tp, d), wo.astype(BF16), mlp_norm[i][None, :],
                     mlp_w1[i].astype(BF16), mlp_w2[i].astype(BF16), tm).reshape(b, tp, d)
    return h[:, N_META:N_META + s]
```

```python
import functools
import math

import jax
import jax.numpy as jnp
from jax import lax
from jax.experimental import pallas as pl
from jax.experimental.pallas import tpu as pltpu

F32 = jnp.float32
BF16 = jnp.bfloat16

D_MODEL = 1024
HEAD_DIM = 64
HALF = HEAD_DIM // 2
N_META = 16
BLOCK = 128
CHUNK = 256
A_Q_HEADS = 16
A_KV_HEADS = 4
A_REP = A_Q_HEADS // A_KV_HEADS
IDX_HEADS = 8
B_HEADS = 16
D_FF = 4 * D_MODEL
FF_CHUNK = 1024
TOPK = 256
EPS = 1e-6
ROPE_THETA = 10000.0
LOG2E = math.log2(math.e)
NEG = -1e30
N_BISECT = 20
IDX_BITS = 12
VROWS = 80
LANES = 128
VMEM_LIMIT = 52 * 1024 * 1024


def _cparams(*sem):
    return pltpu.CompilerParams(dimension_semantics=sem, vmem_limit_bytes=VMEM_LIMIT)


def _dot(a, b):
    return jnp.dot(a, b, preferred_element_type=F32)


def _split2(x):
    hi = x.astype(BF16)
    lo = (x - hi.astype(F32)).astype(BF16)
    return hi, lo


def _split3(x):
    hi = x.astype(BF16)
    r = x - hi.astype(F32)
    mid = r.astype(BF16)
    lo = (r - mid.astype(F32)).astype(BF16)
    return hi, mid, lo


def _rms_rows(h, g):
    ms = jnp.mean(h * h, axis=-1, keepdims=True)
    return h * lax.rsqrt(ms + EPS) * g


def _head_norm_nat(x, e_ref, et_ref, gain):
    x2 = x * x
    hi, lo = _split2(x2)
    ss = _dot(hi, e_ref[...]) + _dot(lo, e_ref[...])
    inv = lax.rsqrt(ss * (1.0 / HEAD_DIM) + EPS)
    ihi, ilo = _split2(inv)
    invx = _dot(ihi, et_ref[...]) + _dot(ilo, et_ref[...])
    return x * invx * gain


def _rope_nat(x, c2, s2):
    w = x.shape[1]
    reps = w // LANES
    c = c2 if reps == 1 else jnp.concatenate([c2] * reps, axis=1)
    s = s2 if reps == 1 else jnp.concatenate([s2] * reps, axis=1)
    lane = lax.broadcasted_iota(jnp.int32, x.shape, 1)
    first = (lane & HALF) == 0
    partner = jnp.where(first, pltpu.roll(x, w - HALF, 1), pltpu.roll(x, HALF, 1))
    return x * c + partner * s


def _proj_a_kernel(h_ref, g_ref, w_ref, e_ref, et_ref, kg_ref, c2_ref, s2_ref,
                   q_o, k_o, v_o, qi_o, ki_o, wi_o):
    hn = _rms_rows(h_ref[0], g_ref[...]).astype(BF16)
    y = _dot(hn, w_ref[...])
    q_o[0] = y[:, 0:1024].astype(BF16)
    c2 = c2_ref[...]
    s2 = s2_ref[...]
    k = _head_norm_nat(y[:, 1024:1280], e_ref, et_ref, kg_ref[...])
    k_o[0] = _rope_nat(k, c2, s2).astype(BF16)
    v_o[0] = y[:, 1280:1536].astype(BF16)
    qi_o[0] = y[:, 1536:2048].astype(BF16)
    kiw = y[:, 2048:2176]
    ki_o[0] = _rope_nat(kiw, c2, s2)[:, 0:HEAD_DIM].astype(BF16)
    wi_o[0] = kiw * (IDX_HEADS ** -0.5 * HEAD_DIM ** -0.5)


def _proj_a(h, g, w_all, e, et, kgain, c2, s2, tm):
    b, tp, d = h.shape
    nt = tp // tm
    wn = w_all.shape[1]
    const = lambda *_: (0, 0)
    tile = lambda bi, j: (bi, j, 0)
    return pl.pallas_call(
        _proj_a_kernel,
        grid=(b, nt),
        in_specs=[
            pl.BlockSpec((1, tm, d), tile),
            pl.BlockSpec((1, d), const),
            pl.BlockSpec((d, wn), const),
            pl.BlockSpec(e.shape, const),
            pl.BlockSpec(et.shape, const),
            pl.BlockSpec((1, 256), const),
            pl.BlockSpec((tm, LANES), lambda bi, j: (j, 0)),
            pl.BlockSpec((tm, LANES), lambda bi, j: (j, 0)),
        ],
        out_specs=[
            pl.BlockSpec((1, tm, 1024), tile),
            pl.BlockSpec((1, tm, 256), tile),
            pl.BlockSpec((1, tm, 256), tile),
            pl.BlockSpec((1, tm, 512), tile),
            pl.BlockSpec((1, tm, HEAD_DIM), tile),
            pl.BlockSpec((1, tm, LANES), tile),
        ],
        out_shape=[
            jax.ShapeDtypeStruct((b, tp, 1024), BF16),
            jax.ShapeDtypeStruct((b, tp, 256), BF16),
            jax.ShapeDtypeStruct((b, tp, 256), BF16),
            jax.ShapeDtypeStruct((b, tp, 512), BF16),
            jax.ShapeDtypeStruct((b, tp, HEAD_DIM), BF16),
            jax.ShapeDtypeStruct((b, tp, LANES), F32),
        ],
        compiler_params=_cparams("parallel", "parallel"),
        name="proj_a",
    )(h, g, w_all, e, et, kgain, c2, s2)


def _head_T(xT, h, cosT, sinT, gain, scale):
    x = xT[HEAD_DIM * h:HEAD_DIM * (h + 1)]
    if gain is not None:
        ss = jnp.sum(x * x, axis=0, keepdims=True)
        x = x * lax.rsqrt(ss * (1.0 / HEAD_DIM) + EPS) * gain
    if cosT is not None:
        x1 = x[:HALF]
        x2 = x[HALF:]
        x = jnp.concatenate([x1 * cosT - x2 * sinT, x1 * sinT + x2 * cosT], axis=0)
    if scale is not None:
        x = x * scale
    return x


def _update_max(m_ref, cm_ref, al_ref):
    m_old = m_ref[...]
    m_new = jnp.maximum(m_old, cm_ref[...])
    al_ref[...] = jnp.exp2(m_old - m_new)
    m_ref[...] = m_new


def _probs(nheads, s_ref, m_ref, p_ref):
    for h in range(nheads):
        p_ref[:, BLOCK * h:BLOCK * (h + 1)] = jnp.exp2(s_ref[h] - m_ref[h:h + 1, :]).astype(BF16)


def _init_softmax(m_ref, acc_ref):
    m_ref[...] = jnp.full(m_ref.shape, NEG, F32)
    acc_ref[...] = jnp.zeros(acc_ref.shape, F32)


def _finish_heads(nheads, acc_ref, o_ref):
    outs = [acc_ref[h, 0:HEAD_DIM, :] / acc_ref[h, HEAD_DIM:HEAD_DIM + 1, :] for h in range(nheads)]
    oT = jnp.concatenate(outs, axis=0)
    o_ref[0] = oT.T.astype(BF16)


def _values_T(v, nheads):
    vT = v.T
    row = lax.broadcasted_iota(jnp.int32, (VROWS - HEAD_DIM, v.shape[0]), 0)
    ones = jnp.where(row == 0, 1.0, 0.0)
    parts = []
    for h in range(nheads):
        parts += [vT[HEAD_DIM * h:HEAD_DIM * (h + 1)], ones]
    return jnp.concatenate(parts, axis=0)


def _valid_rows(c, tp, shape):
    return c * CHUNK + lax.broadcasted_iota(jnp.int32, shape, 0) < tp


def _dsa_kernel(q_ref, k_ref, vt_ref, qi_ref, ki_ref, wi_ref, cos_ref, sin_ref, qg_ref,
                o_ref, sc_ref, qx_ref, qix_ref, thr_ref, cut_ref, s_ref, p_ref, cm_ref, m_ref, al_ref, acc_ref):
    blk = pl.program_id(1)
    nch = (blk + 2) // 2
    cosT = cos_ref[...]
    sinT = sin_ref[...]

    qT = q_ref[0].astype(F32).T
    qscale = HEAD_DIM ** -0.5 * LOG2E
    for g in range(A_KV_HEADS):
        heads = [_head_T(qT, A_REP * g + j, cosT, sinT, qg_ref[...], qscale).astype(BF16)
                 for j in range(A_REP)]
        row = jnp.concatenate(heads, axis=1)
        parts = []
        if g > 0:
            parts.append(jnp.zeros((HEAD_DIM * g, A_REP * BLOCK), BF16))
        parts.append(row)
        if g < A_KV_HEADS - 1:
            parts.append(jnp.zeros((HEAD_DIM * (A_KV_HEADS - 1 - g), A_REP * BLOCK), BF16))
        qx_ref[g] = jnp.concatenate(parts, axis=0)

    qiT = qi_ref[0].astype(F32).T
    qix_ref[...] = jnp.concatenate(
        [_head_T(qiT, h, cosT, sinT, None, None).astype(BF16) for h in range(IDX_HEADS)], axis=1)
    wT = wi_ref[0].T

    qidx = blk * BLOCK + lax.broadcasted_iota(jnp.int32, (CHUNK, BLOCK), 1)
    krow = lax.broadcasted_iota(jnp.int32, (CHUNK, BLOCK), 0)

    def idx_body(c, carry):
        off = pl.multiple_of(c * CHUNK, CHUNK)
        lg = _dot(ki_ref[0, pl.ds(off, CHUNK), :], qix_ref[...])
        acc = jnp.zeros((CHUNK, BLOCK), F32)
        for h in range(IDX_HEADS):
            w_h = wT[HEAD_DIM + h:HEAD_DIM + h + 1, :]
            acc = acc + w_h * jnp.maximum(lg[:, BLOCK * h:BLOCK * (h + 1)], 0.0)
        sc_ref[pl.ds(off, CHUNK), :] = jnp.where(krow + off <= qidx, acc, -jnp.inf)
        return carry

    lax.fori_loop(0, nch, idx_body, 0)

    thr_ref[...] = jnp.full((1, BLOCK), -jnp.inf, F32)
    cut_ref[...] = jnp.full((1, BLOCK), -1, jnp.int32)

    def reduce_chunks(fn, init, combine):
        def body(c, acc):
            off = pl.multiple_of(c * CHUNK, CHUNK)
            val = fn(sc_ref[pl.ds(off, CHUNK), :], off)
            part = val.reshape(CHUNK // 8, 8, BLOCK)
            return combine(acc, part)
        return lax.fori_loop(0, nch, body, init)

    def count(pred_fn):
        acc = reduce_chunks(lambda s, off: jnp.where(pred_fn(s, off), 1.0, 0.0),
                            jnp.zeros((8, BLOCK), F32),
                            lambda a, p: a + jnp.sum(p, axis=0))
        return jnp.sum(acc, axis=0, keepdims=True)

    def masked_max(pred_fn):
        acc = reduce_chunks(lambda s, off: jnp.where(pred_fn(s, off), s, -jnp.inf),
                            jnp.full((8, BLOCK), -jnp.inf, F32),
                            lambda a, p: jnp.maximum(a, jnp.max(p, axis=0)))
        return jnp.max(acc, axis=0, keepdims=True)

    def masked_min(pred_fn):
        acc = reduce_chunks(lambda s, off: jnp.where(pred_fn(s, off), s, jnp.inf),
                            jnp.full((8, BLOCK), jnp.inf, F32),
                            lambda a, p: jnp.minimum(a, jnp.min(p, axis=0)))
        return jnp.min(acc, axis=0, keepdims=True)

    kf = float(TOPK)

    @pl.when(blk >= 2)
    def _():
        causal = lambda s, off: s > -jnp.inf
        vmax = masked_max(causal)
        vmin = masked_min(causal)
        top_full = count(lambda s, off: s >= vmax) >= kf

        def bis(_, lohi):
            lo, hi = lohi
            mid = 0.5 * (lo + hi)
            ok = count(lambda s, off: s >= mid) >= kf
            return jnp.where(ok, mid, lo), jnp.where(ok, hi, mid)

        lo, hi = lax.fori_loop(0, N_BISECT, bis, (vmin, vmax))

        thr0 = masked_max(lambda s, off: s < hi)
        cnt0 = count(lambda s, off: s >= thr0)

        def walk_cond(st):
            thr, cnt = st
            return jnp.max(jnp.where(cnt < kf, 1.0, 0.0)) > 0.0

        def walk_body(st):
            thr, cnt = st
            nxt = masked_max(lambda s, off: s < thr)
            thr = jnp.where(cnt < kf, nxt, thr)
            return thr, count(lambda s, off: s >= thr)

        thr, cnt_ge = lax.while_loop(walk_cond, walk_body, (thr0, cnt0))
        thr = jnp.where(top_full, vmax, thr)
        cnt_ge = jnp.where(top_full, count(lambda s, off: s >= vmax), cnt_ge)
        thr_ref[...] = thr
        cut_ref[...] = jnp.full((1, BLOCK), 2 ** IDX_BITS, jnp.int32)

        @pl.when(jnp.max(cnt_ge) > kf)
        def _():
            need = kf - count(lambda s, off: s > thr)

            def tie_bis(i, lo_i):
                half = lax.shift_right_logical(jnp.int32(2 ** (IDX_BITS - 1)), i)
                mid = lo_i + half
                got = count(lambda s, off: (s == thr) & (krow + off <= mid))
                return jnp.where(got >= need, lo_i, mid)

            lo_i = lax.fori_loop(0, IDX_BITS, tie_bis, jnp.full((1, BLOCK), -1, jnp.int32))
            cut_ref[...] = lo_i + 1

    _init_softmax(m_ref, acc_ref)

    def att_body(c, carry):
        off = pl.multiple_of(c * CHUNK, CHUNK)
        s_idx = sc_ref[pl.ds(off, CHUNK), :]
        thr = thr_ref[...]
        sel = (s_idx > thr) | ((s_idx == thr) & (krow + off <= cut_ref[...]))
        bias = jnp.where(sel, 0.0, NEG)
        kc = k_ref[0, pl.ds(off, CHUNK), :]
        for g in range(A_KV_HEADS):
            sT = _dot(kc, qx_ref[g])
            for j in range(A_REP):
                h = A_REP * g + j
                s = sT[:, BLOCK * j:BLOCK * (j + 1)] + bias
                s_ref[h] = s
                cm_ref[h:h + 1, :] = jnp.max(s, axis=0, keepdims=True)
        _update_max(m_ref, cm_ref, al_ref)
        _probs(A_Q_HEADS, s_ref, m_ref, p_ref)
        for g in range(A_KV_HEADS):
            w = A_REP * BLOCK
            pv = _dot(vt_ref[0, c, VROWS * g:VROWS * (g + 1), :], p_ref[:, w * g:w * (g + 1)])
            for j in range(A_REP):
                h = A_REP * g + j
                acc_ref[h] = al_ref[h:h + 1, :] * acc_ref[h] + pv[:, BLOCK * j:BLOCK * (j + 1)]
        return carry

    lax.fori_loop(0, nch, att_body, 0)
    _finish_heads(A_Q_HEADS, acc_ref, o_ref)


def _dsa_layout_kernel(tp, k_ref, v_ref, ki_ref, k_o, vt_o, ki_o):
    c = pl.program_id(1)
    k_o[0] = jnp.where(_valid_rows(c, tp, k_ref.shape[1:]), k_ref[0], 0)
    ki_o[0] = jnp.where(_valid_rows(c, tp, ki_ref.shape[1:]), ki_ref[0], 0)
    v = jnp.where(_valid_rows(c, tp, v_ref.shape[1:]), v_ref[0].astype(F32), 0.0)
    vt_o[0, 0] = _values_T(v, A_KV_HEADS).astype(BF16)


def _dsa_layout(k, v, ki, tk):
    b, tp, _ = k.shape
    nch = tk // CHUNK
    chunk = lambda bi, c: (bi, c, 0)
    return pl.pallas_call(
        functools.partial(_dsa_layout_kernel, tp),
        grid=(b, nch),
        in_specs=[
            pl.BlockSpec((1, CHUNK, 256), chunk),
            pl.BlockSpec((1, CHUNK, 256), chunk),
            pl.BlockSpec((1, CHUNK, HEAD_DIM), chunk),
        ],
        out_specs=[
            pl.BlockSpec((1, CHUNK, 256), chunk),
            pl.BlockSpec((1, 1, A_KV_HEADS * VROWS, CHUNK), lambda bi, c: (bi, c, 0, 0)),
            pl.BlockSpec((1, CHUNK, HEAD_DIM), chunk),
        ],
        out_shape=[
            jax.ShapeDtypeStruct((b, tk, 256), BF16),
            jax.ShapeDtypeStruct((b, nch, A_KV_HEADS * VROWS, CHUNK), BF16),
            jax.ShapeDtypeStruct((b, tk, HEAD_DIM), BF16),
        ],
        compiler_params=_cparams("parallel", "parallel"),
        name="dsa_layout",
    )(k, v, ki)


def _dsa_attention(q, k, vt, qi, ki, wi, cosT, sinT, qgain):
    b, tp, _ = q.shape
    tk = k.shape[1]
    nblk = tp // BLOCK
    qblk = lambda bi, j: (bi, j, 0)
    seq = lambda bi, j: (bi, 0, 0)
    return pl.pallas_call(
        _dsa_kernel,
        grid=(b, nblk),
        in_specs=[
            pl.BlockSpec((1, BLOCK, 1024), qblk),
            pl.BlockSpec((1, tk, 256), seq),
            pl.BlockSpec((1,) + vt.shape[1:], lambda bi, j: (bi, 0, 0, 0)),
            pl.BlockSpec((1, BLOCK, 512), qblk),
            pl.BlockSpec((1, tk, HEAD_DIM), seq),
            pl.BlockSpec((1, BLOCK, LANES), qblk),
            pl.BlockSpec((HALF, BLOCK), lambda bi, j: (0, j)),
            pl.BlockSpec((HALF, BLOCK), lambda bi, j: (0, j)),
            pl.BlockSpec((HEAD_DIM, 1), lambda bi, j: (0, 0)),
        ],
        out_specs=pl.BlockSpec((1, BLOCK, 1024), qblk),
        out_shape=jax.ShapeDtypeStruct((b, tp, 1024), BF16),
        scratch_shapes=[
            pltpu.VMEM((tk, BLOCK), F32),
            pltpu.VMEM((A_KV_HEADS, 256, A_REP * BLOCK), BF16),
            pltpu.VMEM((HEAD_DIM, IDX_HEADS * BLOCK), BF16),
            pltpu.VMEM((1, BLOCK), F32),
            pltpu.VMEM((1, BLOCK), jnp.int32),
            pltpu.VMEM((A_Q_HEADS, CHUNK, BLOCK), F32),
            pltpu.VMEM((CHUNK, A_Q_HEADS * BLOCK), BF16),
            pltpu.VMEM((A_Q_HEADS, BLOCK), F32),
            pltpu.VMEM((A_Q_HEADS, BLOCK), F32),
            pltpu.VMEM((A_Q_HEADS, BLOCK), F32),
            pltpu.VMEM((A_Q_HEADS, VROWS, BLOCK), F32),
        ],
        compiler_params=_cparams("parallel", "arbitrary"),
        name="dsa_attn",
    )(q, k, vt, qi, ki, wi, cosT, sinT, qgain)


def _mlp_kernel(h_ref, o_ref, wo_ref, g_ref, w1_ref, w2_ref, out_ref):
    h1 = h_ref[...] + _dot(o_ref[...], wo_ref[...])
    hn = _rms_rows(h1, g_ref[...]).astype(BF16)
    acc = h1
    for c in range(D_FF // FF_CHUNK):
        u = jnp.maximum(_dot(hn, w1_ref[:, FF_CHUNK * c:FF_CHUNK * (c + 1)]), 0.0)
        acc = acc + _dot((u * u).astype(BF16), w2_ref[FF_CHUNK * c:FF_CHUNK * (c + 1), :])
    out_ref[...] = acc


def _out_mlp(h2, o2, wo, g, w1, w2, tm):
    n, d = h2.shape
    const = lambda i: (0, 0)
    tile = lambda i: (i, 0)
    once = pl.Buffered(1)
    return pl.pallas_call(
        _mlp_kernel,
        grid=(n // tm,),
        in_specs=[
            pl.BlockSpec((tm, d), tile),
            pl.BlockSpec((tm, d), tile),
            pl.BlockSpec((d, d), const, pipeline_mode=once),
            pl.BlockSpec((1, d), const),
            pl.BlockSpec((d, D_FF), const, pipeline_mode=once),
            pl.BlockSpec((D_FF, d), const, pipeline_mode=once),
        ],
        out_specs=pl.BlockSpec((tm, d), tile),
        out_shape=jax.ShapeDtypeStruct((n, d), F32),
        compiler_params=_cparams("parallel"),
        name="out_mlp",
    )(h2, o2, wo, g, w1, w2)


def _shared_kv_kernel(h_ref, g_ref, w_ref, e_ref, et_ref, kg_ref, fb_ref, k_o, v_o, lf_o):
    hn = _rms_rows(h_ref[0], g_ref[...]).astype(BF16)
    y = _dot(hn, w_ref[...])
    k_o[0] = _head_norm_nat(y[:, 0:1024], e_ref, et_ref, kg_ref[...]).astype(BF16)
    v_o[0] = y[:, 1024:2048].astype(BF16)
    x = y[:, 2048:2176] + fb_ref[...]
    lf_o[0] = jnp.minimum(x, 0.0) - jnp.log1p(jnp.exp(-jnp.abs(x)))


def _shared_kv(h, g, w_all, e, et, kgain, fbias, tm):
    b, tp, d = h.shape
    const = lambda *_: (0, 0)
    tile = lambda bi, j: (bi, j, 0)
    return pl.pallas_call(
        _shared_kv_kernel,
        grid=(b, tp // tm),
        in_specs=[
            pl.BlockSpec((1, tm, d), tile),
            pl.BlockSpec((1, d), const),
            pl.BlockSpec(w_all.shape, const),
            pl.BlockSpec(e.shape, const),
            pl.BlockSpec(et.shape, const),
            pl.BlockSpec((1, 1024), const),
            pl.BlockSpec((1, LANES), const),
        ],
        out_specs=[
            pl.BlockSpec((1, tm, 1024), tile),
            pl.BlockSpec((1, tm, 1024), tile),
            pl.BlockSpec((1, tm, LANES), tile),
        ],
        out_shape=[
            jax.ShapeDtypeStruct((b, tp, 1024), BF16),
            jax.ShapeDtypeStruct((b, tp, 1024), BF16),
            jax.ShapeDtypeStruct((b, tp, LANES), F32),
        ],
        compiler_params=_cparams("parallel", "parallel"),
        name="shared_kv",
    )(h, g, w_all, e, et, kgain, fbias)


def _cumsum_kernel(lf_ref, c_ref):
    r = lax.broadcasted_iota(jnp.int32, (BLOCK, BLOCK), 0)
    cidx = lax.broadcasted_iota(jnp.int32, (BLOCK, BLOCK), 1)
    tri = jnp.where(cidx <= r, 1.0, 0.0).astype(BF16)
    ntile = lf_ref.shape[1] // BLOCK

    def body(i, carry):
        off = pl.multiple_of(i * BLOCK, BLOCK)
        hi, mid, lo = _split3(lf_ref[0, pl.ds(off, BLOCK), :])
        cs = _dot(tri, hi) + _dot(tri, mid) + _dot(tri, lo) + carry
        c_ref[0, pl.ds(off, BLOCK), :] = cs
        return cs[BLOCK - 1:BLOCK, :]

    lax.fori_loop(0, ntile, body, jnp.zeros((1, LANES), F32))


def _cumsum_t(lf):
    b, tp, w = lf.shape
    spec = pl.BlockSpec((1, tp, w), lambda bi: (bi, 0, 0))
    return pl.pallas_call(
        _cumsum_kernel,
        grid=(b,),
        in_specs=[spec],
        out_specs=spec,
        out_shape=jax.ShapeDtypeStruct((b, tp, w), F32),
        compiler_params=_cparams("parallel"),
        name="cumsum_t",
    )(lf)


def _proj_q_kernel(h_ref, g_ref, w_ref, q_o):
    hn = _rms_rows(h_ref[...], g_ref[...]).astype(BF16)
    q_o[...] = _dot(hn, w_ref[...]).astype(BF16)


def _proj_q(h2, g, w, tm):
    n, d = h2.shape
    const = lambda i: (0, 0)
    tile = lambda i: (i, 0)
    return pl.pallas_call(
        _proj_q_kernel,
        grid=(n // tm,),
        in_specs=[pl.BlockSpec((tm, d), tile), pl.BlockSpec((1, d), const), pl.BlockSpec((d, d), const)],
        out_specs=pl.BlockSpec((tm, d), tile),
        out_shape=jax.ShapeDtypeStruct((n, d), BF16),
        compiler_params=_cparams("parallel"),
        name="proj_q",
    )(h2, g, w)


def _fox_layout_kernel(tp, k_ref, v_ref, c_ref, pk_ref, pc_ref, ka_o, vt_o):
    ci = pl.program_id(1)
    k = jnp.where(_valid_rows(ci, tp, k_ref.shape[1:]), k_ref[0], 0)
    cl = jnp.where(_valid_rows(ci, tp, c_ref.shape[1:]), c_ref[0] * LOG2E, 0.0)
    hi, mid, lo = _split3(cl)
    ka = _dot(k, pk_ref[...]) + _dot(hi, pc_ref[0]) + _dot(mid, pc_ref[1]) + _dot(lo, pc_ref[2])
    ka_o[0] = ka.astype(BF16)
    v = jnp.where(_valid_rows(ci, tp, v_ref.shape[1:]), v_ref[0].astype(F32), 0.0)
    vt_o[0, 0] = _values_T(v, B_HEADS).astype(BF16)


def _fox_layout(k, v, c, pk, pc, tk):
    b, tp, _ = k.shape
    nch = tk // CHUNK
    chunk = lambda bi, ci: (bi, ci, 0)
    return pl.pallas_call(
        functools.partial(_fox_layout_kernel, tp),
        grid=(b, nch),
        in_specs=[
            pl.BlockSpec((1, CHUNK, 1024), chunk),
            pl.BlockSpec((1, CHUNK, 1024), chunk),
            pl.BlockSpec((1, CHUNK, LANES), chunk),
            pl.BlockSpec(pk.shape, lambda bi, ci: (0, 0)),
            pl.BlockSpec(pc.shape, lambda bi, ci: (0, 0, 0)),
        ],
        out_specs=[
            pl.BlockSpec((1, CHUNK, B_HEADS * LANES), chunk),
            pl.BlockSpec((1, 1, B_HEADS * VROWS, CHUNK), lambda bi, ci: (bi, ci, 0, 0)),
        ],
        out_shape=[
            jax.ShapeDtypeStruct((b, tk, B_HEADS * LANES), BF16),
            jax.ShapeDtypeStruct((b, nch, B_HEADS * VROWS, CHUNK), BF16),
        ],
        compiler_params=_cparams("parallel", "parallel"),
        name="fox_layout",
    )(k, v, c, pk, pc)


def _fox_kernel(q_ref, ka_ref, vt_ref, qg_ref, o_ref, qx_ref, s_ref, p_ref, cm_ref, m_ref, al_ref, acc_ref):
    blk = pl.program_id(1)
    nch = (blk + 2) // 2
    qT = q_ref[0].astype(F32).T
    qscale = HEAD_DIM ** -0.5 * LOG2E
    row = lax.broadcasted_iota(jnp.int32, (HEAD_DIM, BLOCK), 0)
    minus3 = jnp.where(row < 3, -1.0, 0.0).astype(BF16)
    zeros = jnp.zeros((LANES, BLOCK), BF16)
    for pr in range(B_HEADS // 2):
        xa = _head_T(qT, 2 * pr, None, None, qg_ref[...], qscale).astype(BF16)
        xb = _head_T(qT, 2 * pr + 1, None, None, qg_ref[...], qscale).astype(BF16)
        left = jnp.concatenate([xa, minus3, zeros], axis=0)
        right = jnp.concatenate([zeros, xb, minus3], axis=0)
        qx_ref[pr] = jnp.concatenate([left, right], axis=1)

    _init_softmax(m_ref, acc_ref)
    qidx = blk * BLOCK + lax.broadcasted_iota(jnp.int32, (CHUNK, BLOCK), 1)
    krow = lax.broadcasted_iota(jnp.int32, (CHUNK, BLOCK), 0)

    def att_body(c, carry):
        off = pl.multiple_of(c * CHUNK, CHUNK)
        bias = jnp.where(krow + off <= qidx, 0.0, NEG)
        for pr in range(B_HEADS // 2):
            sT = _dot(ka_ref[0, pl.ds(off, CHUNK), 2 * LANES * pr:2 * LANES * (pr + 1)], qx_ref[pr])
            for j in range(2):
                h = 2 * pr + j
                s = sT[:, BLOCK * j:BLOCK * (j + 1)] + bias
                s_ref[h] = s
                cm_ref[h:h + 1, :] = jnp.max(s, axis=0, keepdims=True)
        _update_max(m_ref, cm_ref, al_ref)
        _probs(B_HEADS, s_ref, m_ref, p_ref)
        for pr in range(B_HEADS // 2):
            pv = _dot(vt_ref[0, c, 2 * VROWS * pr:2 * VROWS * (pr + 1), :],
                      p_ref[:, 2 * BLOCK * pr:2 * BLOCK * (pr + 1)])
            for j in range(2):
                h = 2 * pr + j
                acc_ref[h] = (al_ref[h:h + 1, :] * acc_ref[h]
                              + pv[VROWS * j:VROWS * (j + 1), BLOCK * j:BLOCK * (j + 1)])
        return carry

    lax.fori_loop(0, nch, att_body, 0)
    _finish_heads(B_HEADS, acc_ref, o_ref)


def _fox_attention(q, ka, vt, qgain):
    b, tp, _ = q.shape
    tk = ka.shape[1]
    qblk = lambda bi, j: (bi, j, 0)
    return pl.pallas_call(
        _fox_kernel,
        grid=(b, tp // BLOCK),
        in_specs=[
            pl.BlockSpec((1, BLOCK, 1024), qblk),
            pl.BlockSpec((1, tk, B_HEADS * LANES), lambda bi, j: (bi, 0, 0)),
            pl.BlockSpec((1,) + vt.shape[1:], lambda bi, j: (bi, 0, 0, 0)),
            pl.BlockSpec((HEAD_DIM, 1), lambda bi, j: (0, 0)),
        ],
        out_specs=pl.BlockSpec((1, BLOCK, 1024), qblk),
        out_shape=jax.ShapeDtypeStruct((b, tp, 1024), BF16),
        scratch_shapes=[
            pltpu.VMEM((B_HEADS // 2, 2 * LANES, 2 * BLOCK), BF16),
            pltpu.VMEM((B_HEADS, CHUNK, BLOCK), F32),
            pltpu.VMEM((CHUNK, B_HEADS * BLOCK), BF16),
            pltpu.VMEM((B_HEADS, BLOCK), F32),
            pltpu.VMEM((B_HEADS, BLOCK), F32),
            pltpu.VMEM((B_HEADS, BLOCK), F32),
            pltpu.VMEM((B_HEADS, VROWS, BLOCK), F32),
        ],
        compiler_params=_cparams("parallel", "arbitrary"),
        name="fox_attn",
    )(q, ka, vt, qgain)


def _indicator(nheads, width):
    e = (jnp.arange(nheads * HEAD_DIM)[:, None] // HEAD_DIM == jnp.arange(width)[None, :])
    return e.astype(BF16), e.T.astype(BF16)


def _pad_cols(w, n):
    return jnp.pad(w, ((0, 0), (0, n - w.shape[1])))


def _placements():
    src = jnp.arange(B_HEADS * HEAD_DIM)
    dst = jnp.arange(B_HEADS * LANES)
    pk = (dst[None, :] == (src[:, None] // HEAD_DIM) * LANES + src[:, None] % HEAD_DIM).astype(BF16)
    head = jnp.arange(LANES)
    pc = jnp.stack([(dst[None, :] == head[:, None] * LANES + HEAD_DIM + i) & (head[:, None] < B_HEADS)
                    for i in range(3)]).astype(BF16)
    return pk, pc


def kernel(x, meta_tokens, attn_norm, mlp_norm, mlp_w1, mlp_w2, a_w_in, a_q_gain, a_k_gain, a_w_out,
           kv_norm, kv_w, kv_f_bias, kv_k_gain, b_w_q, b_q_gain, b_w_out):
    b, s, d = x.shape
    t = s + N_META
    tp = -(-t // BLOCK) * BLOCK
    tk = -(-tp // CHUNK) * CHUNK
    tm = tp // 4
    depth = attn_norm.shape[0]
    n_a = a_w_in.shape[0]
    assert d == D_MODEL and tm % 16 == 0 and min(TOPK, s // 4) == TOPK and tk <= 2 ** IDX_BITS

    meta = jnp.broadcast_to(meta_tokens.astype(x.dtype)[None], (b, N_META, d))
    h = jnp.pad(jnp.concatenate([meta, x], axis=1), ((0, 0), (0, tp - t), (0, 0)))

    inv = 1.0 / (ROPE_THETA ** (jnp.arange(0, HEAD_DIM, 2, dtype=F32) / HEAD_DIM))
    ang = jnp.arange(tp, dtype=F32)[:, None] * inv[None, :]
    cos, sin = jnp.cos(ang), jnp.sin(ang)
    c2 = jnp.concatenate([cos, cos, cos, cos], axis=1)
    s2 = jnp.concatenate([-sin, sin, -sin, sin], axis=1)
    cosT, sinT = cos.T, sin.T

    e4, et4 = _indicator(A_KV_HEADS, LANES)
    e16, et16 = _indicator(B_HEADS, LANES)

    shared = None
    for i in range(depth):
        g_attn = attn_norm[i][None, :]
        if i < n_a:
            w_all = _pad_cols(a_w_in[i], 2176).astype(BF16)
            kgain = jnp.tile(a_k_gain[i], A_KV_HEADS)[None, :]
            q, k, v, qi, ki, wi = _proj_a(h, g_attn, w_all, e4, et4, kgain, c2, s2, tm)
            kp, vt, kip = _dsa_layout(k, v, ki, tk)
            o = _dsa_attention(q, kp, vt, qi, kip, wi, cosT, sinT, a_q_gain[i][:, None])
            wo = a_w_out[i]
        else:
            if shared is None:
                w_all = _pad_cols(kv_w, 2176).astype(BF16)
                kgain = jnp.tile(kv_k_gain, B_HEADS)[None, :]
                fbias = jnp.pad(kv_f_bias, (0, LANES - B_HEADS))[None, :]
                k, v, lf = _shared_kv(h, kv_norm[None, :], w_all, e16, et16, kgain, fbias, tm)
                pk, pc = _placements()
                shared = _fox_layout(k, v, _cumsum_t(lf), pk, pc, tk)
            j = i - n_a
            q = _proj_q(h.reshape(b * tp, d), g_attn, b_w_q[j].astype(BF16), tm).reshape(b, tp, d)
            o = _fox_attention(q, shared[0], shared[1], b_q_gain[j][:, None])
            wo = b_w_out[j]
        h = _out_mlp(h.reshape(b * tp, d), o.reshape(b * tp, d), wo.astype(BF16), mlp_norm[i][None, :],
                     mlp_w1[i].astype(BF16), mlp_w2[i].astype(BF16), tm).reshape(b, tp, d)
    return h[:, N_META:N_META + s]
```

```python
import functools
import math

import jax
import jax.numpy as jnp
from jax import lax
from jax.experimental import pallas as pl
from jax.experimental.pallas import tpu as pltpu

F32 = jnp.float32
BF16 = jnp.bfloat16

D_MODEL = 1024
HEAD_DIM = 64
HALF = HEAD_DIM // 2
N_META = 16
BLOCK = 128
CHUNK = 256
PASS_ROWS = 2 * CHUNK
FOLD = 64
A_Q_HEADS = 16
A_KV_HEADS = 4
A_REP = A_Q_HEADS // A_KV_HEADS
IDX_HEADS = 8
B_HEADS = 16
D_FF = 4 * D_MODEL
FF_CHUNK = 1024
TOPK = 256
EPS = 1e-6
ROPE_THETA = 10000.0
LOG2E = math.log2(math.e)
NEG = -1e30
N_BISECT = 20
IDX_BITS = 12
VROWS = 80
LANES = 128
VMEM_LIMIT = 52 * 1024 * 1024


def _cparams(*sem):
    return pltpu.CompilerParams(dimension_semantics=sem, vmem_limit_bytes=VMEM_LIMIT)


def _dot(a, b):
    return jnp.dot(a, b, preferred_element_type=F32)


def _split2(x):
    hi = x.astype(BF16)
    lo = (x - hi.astype(F32)).astype(BF16)
    return hi, lo


def _split3(x):
    hi = x.astype(BF16)
    r = x - hi.astype(F32)
    mid = r.astype(BF16)
    lo = (r - mid.astype(F32)).astype(BF16)
    return hi, mid, lo


def _rms_rows(h, g):
    ms = jnp.mean(h * h, axis=-1, keepdims=True)
    return h * lax.rsqrt(ms + EPS) * g


def _head_norm_nat(x, e_ref, et_ref, gain):
    x2 = x * x
    hi, lo = _split2(x2)
    ss = _dot(hi, e_ref[...]) + _dot(lo, e_ref[...])
    inv = lax.rsqrt(ss * (1.0 / HEAD_DIM) + EPS)
    ihi, ilo = _split2(inv)
    invx = _dot(ihi, et_ref[...]) + _dot(ilo, et_ref[...])
    return x * invx * gain


def _rope_nat(x, c2, s2):
    w = x.shape[1]
    reps = w // LANES
    c = c2 if reps == 1 else jnp.concatenate([c2] * reps, axis=1)
    s = s2 if reps == 1 else jnp.concatenate([s2] * reps, axis=1)
    lane = lax.broadcasted_iota(jnp.int32, x.shape, 1)
    first = (lane & HALF) == 0
    partner = jnp.where(first, pltpu.roll(x, w - HALF, 1), pltpu.roll(x, HALF, 1))
    return x * c + partner * s


def _proj_a_kernel(h_ref, g_ref, w_ref, e_ref, et_ref, kg_ref, c2_ref, s2_ref,
                   q_o, k_o, v_o, qi_o, ki_o, wi_o):
    hn = _rms_rows(h_ref[0], g_ref[...]).astype(BF16)
    y = _dot(hn, w_ref[...])
    q_o[0] = y[:, 0:1024].astype(BF16)
    c2 = c2_ref[...]
    s2 = s2_ref[...]
    k = _head_norm_nat(y[:, 1024:1280], e_ref, et_ref, kg_ref[...])
    k_o[0] = _rope_nat(k, c2, s2).astype(BF16)
    v_o[0] = y[:, 1280:1536].astype(BF16)
    qi_o[0] = y[:, 1536:2048].astype(BF16)
    kiw = y[:, 2048:2176]
    ki_o[0] = _rope_nat(kiw, c2, s2)[:, 0:HEAD_DIM].astype(BF16)
    wi_o[0] = kiw * (IDX_HEADS ** -0.5 * HEAD_DIM ** -0.5)


def _proj_a(h, g, w_all, e, et, kgain, c2, s2, tm):
    b, tp, d = h.shape
    nt = tp // tm
    wn = w_all.shape[1]
    const = lambda *_: (0, 0)
    tile = lambda bi, j: (bi, j, 0)
    return pl.pallas_call(
        _proj_a_kernel,
        grid=(b, nt),
        in_specs=[
            pl.BlockSpec((1, tm, d), tile),
            pl.BlockSpec((1, d), const),
            pl.BlockSpec((d, wn), const),
            pl.BlockSpec(e.shape, const),
            pl.BlockSpec(et.shape, const),
            pl.BlockSpec((1, 256), const),
            pl.BlockSpec((tm, LANES), lambda bi, j: (j, 0)),
            pl.BlockSpec((tm, LANES), lambda bi, j: (j, 0)),
        ],
        out_specs=[
            pl.BlockSpec((1, tm, 1024), tile),
            pl.BlockSpec((1, tm, 256), tile),
            pl.BlockSpec((1, tm, 256), tile),
            pl.BlockSpec((1, tm, 512), tile),
            pl.BlockSpec((1, tm, HEAD_DIM), tile),
            pl.BlockSpec((1, tm, LANES), tile),
        ],
        out_shape=[
            jax.ShapeDtypeStruct((b, tp, 1024), BF16),
            jax.ShapeDtypeStruct((b, tp, 256), BF16),
            jax.ShapeDtypeStruct((b, tp, 256), BF16),
            jax.ShapeDtypeStruct((b, tp, 512), BF16),
            jax.ShapeDtypeStruct((b, tp, HEAD_DIM), BF16),
            jax.ShapeDtypeStruct((b, tp, LANES), F32),
        ],
        compiler_params=_cparams("parallel", "parallel"),
        name="proj_a",
    )(h, g, w_all, e, et, kgain, c2, s2)


def _head_T(xT, h, cosT, sinT, gain, scale):
    x = xT[HEAD_DIM * h:HEAD_DIM * (h + 1)]
    if gain is not None:
        ss = jnp.sum(x * x, axis=0, keepdims=True)
        x = x * lax.rsqrt(ss * (1.0 / HEAD_DIM) + EPS) * gain
    if cosT is not None:
        x1 = x[:HALF]
        x2 = x[HALF:]
        x = jnp.concatenate([x1 * cosT - x2 * sinT, x1 * sinT + x2 * cosT], axis=0)
    if scale is not None:
        x = x * scale
    return x


def _update_max(m_ref, cm_ref, al_ref):
    m_old = m_ref[...]
    m_new = jnp.maximum(m_old, cm_ref[...])
    al_ref[...] = jnp.exp2(m_old - m_new)
    m_ref[...] = m_new


def _probs(nheads, s_ref, m_ref, p_ref):
    for h in range(nheads):
        p_ref[:, BLOCK * h:BLOCK * (h + 1)] = jnp.exp2(s_ref[h] - m_ref[h:h + 1, :]).astype(BF16)


def _init_softmax(m_ref, acc_ref):
    m_ref[...] = jnp.full(m_ref.shape, NEG, F32)
    acc_ref[...] = jnp.zeros(acc_ref.shape, F32)


def _finish_heads(nheads, acc_ref, o_ref):
    outs = [acc_ref[h, 0:HEAD_DIM, :] / acc_ref[h, HEAD_DIM:HEAD_DIM + 1, :] for h in range(nheads)]
    oT = jnp.concatenate(outs, axis=0)
    o_ref[0] = oT.T.astype(BF16)


def _values_T(v, nheads):
    vT = v.T
    row = lax.broadcasted_iota(jnp.int32, (VROWS - HEAD_DIM, v.shape[0]), 0)
    ones = jnp.where(row == 0, 1.0, 0.0)
    parts = []
    for h in range(nheads):
        parts += [vT[HEAD_DIM * h:HEAD_DIM * (h + 1)], ones]
    return jnp.concatenate(parts, axis=0)


def _valid_rows(c, tp, shape):
    return c * CHUNK + lax.broadcasted_iota(jnp.int32, shape, 0) < tp


def _dsa_kernel(q_ref, k_ref, vt_ref, qi_ref, ki_ref, wi_ref, cos_ref, sin_ref, qg_ref,
                o_ref, sc_ref, qx_ref, qix_ref, thr_ref, cut_ref, bias_ref, s_ref, p_ref, cm_ref, m_ref, al_ref,
                acc_ref):
    blk = pl.program_id(1)
    nch = (blk + 2) // 2
    cosT = cos_ref[...]
    sinT = sin_ref[...]

    qT = q_ref[0].astype(F32).T
    qscale = HEAD_DIM ** -0.5 * LOG2E
    for g in range(A_KV_HEADS):
        heads = [_head_T(qT, A_REP * g + j, cosT, sinT, qg_ref[...], qscale).astype(BF16)
                 for j in range(A_REP)]
        row = jnp.concatenate(heads, axis=1)
        parts = []
        if g > 0:
            parts.append(jnp.zeros((HEAD_DIM * g, A_REP * BLOCK), BF16))
        parts.append(row)
        if g < A_KV_HEADS - 1:
            parts.append(jnp.zeros((HEAD_DIM * (A_KV_HEADS - 1 - g), A_REP * BLOCK), BF16))
        qx_ref[g] = jnp.concatenate(parts, axis=0)

    qiT = qi_ref[0].astype(F32).T
    qix_ref[...] = jnp.concatenate(
        [_head_T(qiT, h, cosT, sinT, None, None).astype(BF16) for h in range(IDX_HEADS)], axis=1)
    wT = wi_ref[0].T

    qidx = blk * BLOCK + lax.broadcasted_iota(jnp.int32, (CHUNK, BLOCK), 1)
    krow = lax.broadcasted_iota(jnp.int32, (CHUNK, BLOCK), 0)

    def idx_body(c, carry):
        mx, mn = carry
        off = pl.multiple_of(c * CHUNK, CHUNK)
        lg = _dot(ki_ref[0, pl.ds(off, CHUNK), :], qix_ref[...])
        acc = jnp.zeros((CHUNK, BLOCK), F32)
        for h in range(IDX_HEADS):
            w_h = wT[HEAD_DIM + h:HEAD_DIM + h + 1, :]
            acc = acc + w_h * jnp.maximum(lg[:, BLOCK * h:BLOCK * (h + 1)], 0.0)
        causal = krow + off <= qidx
        hi_part = jnp.where(causal, acc, -jnp.inf)
        lo_part = jnp.where(causal, acc, jnp.inf)
        sc_ref[pl.ds(off, CHUNK), :] = hi_part
        mx = jnp.maximum(mx, jnp.max(hi_part.reshape(CHUNK // FOLD, FOLD, BLOCK), axis=0))
        mn = jnp.minimum(mn, jnp.min(lo_part.reshape(CHUNK // FOLD, FOLD, BLOCK), axis=0))
        return mx, mn

    mx, mn = lax.fori_loop(0, nch, idx_body, (jnp.full((FOLD, BLOCK), -jnp.inf, F32),
                                              jnp.full((FOLD, BLOCK), jnp.inf, F32)))

    @pl.when(nch % 2 == 1)
    def _():
        sc_ref[pl.ds(pl.multiple_of(nch * CHUNK, CHUNK), CHUNK), :] = jnp.full((CHUNK, BLOCK), -jnp.inf, F32)

    thr_ref[...] = jnp.full((1, BLOCK), -jnp.inf, F32)
    cut_ref[...] = jnp.full((1, BLOCK), -1, jnp.int32)

    npass = (nch + 1) // 2
    prow = lax.broadcasted_iota(jnp.int32, (PASS_ROWS, BLOCK), 0)

    def sweep(fn, inits, combine, finish):
        def body(c, accs):
            off = pl.multiple_of(c * PASS_ROWS, PASS_ROWS)
            vals = fn(sc_ref[pl.ds(off, PASS_ROWS), :], off)
            return tuple(combine(a, v.reshape(PASS_ROWS // FOLD, FOLD, BLOCK)) for a, v in zip(accs, vals))
        return tuple(finish(a) for a in lax.fori_loop(0, npass, body, inits))

    def counts(*preds):
        return sweep(lambda s, off: tuple(jnp.where(p(s, off), 1.0, 0.0) for p in preds),
                     (jnp.zeros((FOLD, BLOCK), F32),) * len(preds),
                     lambda a, v: a + jnp.sum(v, axis=0),
                     lambda a: jnp.sum(a, axis=0, keepdims=True))

    def min_above(t):
        return sweep(lambda s, off: (jnp.where(s > t, s, jnp.inf),),
                     (jnp.full((FOLD, BLOCK), jnp.inf, F32),),
                     lambda a, v: jnp.minimum(a, jnp.min(v, axis=0)),
                     lambda a: jnp.min(a, axis=0, keepdims=True))[0]

    kf = float(TOPK)

    @pl.when(blk >= 2)
    def _():
        vmax = jnp.max(mx, axis=0, keepdims=True)
        vmin = jnp.min(mn, axis=0, keepdims=True)
        below = vmin - (jnp.abs(vmin) * 1e-3 + 1e-30)

        def bis(_, lohi):
            lo, hi = lohi
            mid = 0.5 * (lo + hi)
            ok = counts(lambda s, off: s > mid)[0] >= kf
            return jnp.where(ok, mid, lo), jnp.where(ok, hi, mid)

        lo, hi = lax.fori_loop(0, N_BISECT, bis, (below, vmax))

        def probe(lo):
            cand = min_above(lo)
            return (cand,) + counts(lambda s, off: s > cand, lambda s, off: s >= cand)

        def walk_cond(st):
            return jnp.max(jnp.where(st[2] >= kf, 1.0, 0.0)) > 0.0

        def walk_body(st):
            lo, cand, gt, _ = st
            lo = jnp.where(gt >= kf, cand, lo)
            return (lo,) + probe(lo)

        _, thr, cnt_gt, cnt_ge = lax.while_loop(walk_cond, walk_body, (lo,) + probe(lo))
        thr_ref[...] = thr
        cut_ref[...] = jnp.full((1, BLOCK), 2 ** IDX_BITS, jnp.int32)

        @pl.when(jnp.max(cnt_ge) > kf)
        def _():
            need = kf - cnt_gt

            def tie_bis(i, lo_i):
                half = lax.shift_right_logical(jnp.int32(2 ** (IDX_BITS - 1)), i)
                mid = lo_i + half
                got = counts(lambda s, off: (s == thr) & (prow + off <= mid))[0]
                return jnp.where(got >= need, lo_i, mid)

            lo_i = lax.fori_loop(0, IDX_BITS, tie_bis, jnp.full((1, BLOCK), -1, jnp.int32))
            cut_ref[...] = lo_i + 1

    _init_softmax(m_ref, acc_ref)

    def chunk_off(c):
        return pl.multiple_of(jnp.minimum(c, nch - 1) * CHUNK, CHUNK)

    def set_bias(c):
        off = chunk_off(c)
        s_idx = sc_ref[pl.ds(off, CHUNK), :]
        thr = thr_ref[...]
        sel = (s_idx > thr) | ((s_idx == thr) & (krow + off <= cut_ref[...]))
        bias_ref[...] = jnp.where(sel, 0.0, NEG)

    def scores(c, g, s_dst, cm_dst):
        sT = _dot(k_ref[0, pl.ds(chunk_off(c), CHUNK), :], qx_ref[g])
        for j in range(A_REP):
            h = A_REP * g + j
            s = sT[:, BLOCK * j:BLOCK * (j + 1)] + bias_ref[...]
            s_dst[h] = s
            cm_dst[h:h + 1, :] = jnp.max(s, axis=0, keepdims=True)

    def probs(g, s_src):
        for h in range(A_REP * g, A_REP * (g + 1)):
            p_ref[:, BLOCK * h:BLOCK * (h + 1)] = jnp.exp2(s_src[h] - m_ref[h:h + 1, :]).astype(BF16)

    def values(c):
        w = A_REP * BLOCK
        for g in range(A_KV_HEADS):
            pv = _dot(vt_ref[0, c, VROWS * g:VROWS * (g + 1), :], p_ref[:, w * g:w * (g + 1)])
            for j in range(A_REP):
                h = A_REP * g + j
                acc_ref[h] = al_ref[h:h + 1, :] * acc_ref[h] + pv[:, BLOCK * j:BLOCK * (j + 1)]

    def step(c, src, dst):
        _update_max(m_ref, src[1], al_ref)
        if dst is not None:
            set_bias(c + 1)
        for g in range(A_KV_HEADS):
            if dst is not None:
                scores(c + 1, g, *dst)
            probs(g, src[0])
        values(c)

    buf_a = (s_ref.at[0], cm_ref.at[0])
    buf_b = (s_ref.at[1], cm_ref.at[1])
    set_bias(0)
    for g in range(A_KV_HEADS):
        scores(0, g, *buf_a)

    def pair_body(i, carry):
        step(2 * i, buf_a, buf_b)
        step(2 * i + 1, buf_b, buf_a)
        return carry

    lax.fori_loop(0, nch // 2, pair_body, 0)

    @pl.when(nch % 2 == 1)
    def _():
        step(nch - 1, buf_a, None)

    _finish_heads(A_Q_HEADS, acc_ref, o_ref)


def _dsa_layout_kernel(tp, k_ref, v_ref, ki_ref, k_o, vt_o, ki_o):
    c = pl.program_id(1)
    k_o[0] = jnp.where(_valid_rows(c, tp, k_ref.shape[1:]), k_ref[0], 0)
    ki_o[0] = jnp.where(_valid_rows(c, tp, ki_ref.shape[1:]), ki_ref[0], 0)
    v = jnp.where(_valid_rows(c, tp, v_ref.shape[1:]), v_ref[0].astype(F32), 0.0)
    vt_o[0, 0] = _values_T(v, A_KV_HEADS).astype(BF16)


def _dsa_layout(k, v, ki, tk):
    b, tp, _ = k.shape
    nch = tk // CHUNK
    chunk = lambda bi, c: (bi, c, 0)
    return pl.pallas_call(
        functools.partial(_dsa_layout_kernel, tp),
        grid=(b, nch),
        in_specs=[
            pl.BlockSpec((1, CHUNK, 256), chunk),
            pl.BlockSpec((1, CHUNK, 256), chunk),
            pl.BlockSpec((1, CHUNK, HEAD_DIM), chunk),
        ],
        out_specs=[
            pl.BlockSpec((1, CHUNK, 256), chunk),
            pl.BlockSpec((1, 1, A_KV_HEADS * VROWS, CHUNK), lambda bi, c: (bi, c, 0, 0)),
            pl.BlockSpec((1, CHUNK, HEAD_DIM), chunk),
        ],
        out_shape=[
            jax.ShapeDtypeStruct((b, tk, 256), BF16),
            jax.ShapeDtypeStruct((b, nch, A_KV_HEADS * VROWS, CHUNK), BF16),
            jax.ShapeDtypeStruct((b, tk, HEAD_DIM), BF16),
        ],
        compiler_params=_cparams("parallel", "parallel"),
        name="dsa_layout",
    )(k, v, ki)


def _dsa_attention(q, k, vt, qi, ki, wi, cosT, sinT, qgain):
    b, tp, _ = q.shape
    tk = k.shape[1]
    nblk = tp // BLOCK
    qblk = lambda bi, j: (bi, j, 0)
    seq = lambda bi, j: (bi, 0, 0)
    return pl.pallas_call(
        _dsa_kernel,
        grid=(b, nblk),
        in_specs=[
            pl.BlockSpec((1, BLOCK, 1024), qblk),
            pl.BlockSpec((1, tk, 256), seq),
            pl.BlockSpec((1,) + vt.shape[1:], lambda bi, j: (bi, 0, 0, 0)),
            pl.BlockSpec((1, BLOCK, 512), qblk),
            pl.BlockSpec((1, tk, HEAD_DIM), seq),
            pl.BlockSpec((1, BLOCK, LANES), qblk),
            pl.BlockSpec((HALF, BLOCK), lambda bi, j: (0, j)),
            pl.BlockSpec((HALF, BLOCK), lambda bi, j: (0, j)),
            pl.BlockSpec((HEAD_DIM, 1), lambda bi, j: (0, 0)),
        ],
        out_specs=pl.BlockSpec((1, BLOCK, 1024), qblk),
        out_shape=jax.ShapeDtypeStruct((b, tp, 1024), BF16),
        scratch_shapes=[
            pltpu.VMEM((-(-tk // PASS_ROWS) * PASS_ROWS, BLOCK), F32),
            pltpu.VMEM((A_KV_HEADS, 256, A_REP * BLOCK), BF16),
            pltpu.VMEM((HEAD_DIM, IDX_HEADS * BLOCK), BF16),
            pltpu.VMEM((1, BLOCK), F32),
            pltpu.VMEM((1, BLOCK), jnp.int32),
            pltpu.VMEM((CHUNK, BLOCK), F32),
            pltpu.VMEM((2, A_Q_HEADS, CHUNK, BLOCK), F32),
            pltpu.VMEM((CHUNK, A_Q_HEADS * BLOCK), BF16),
            pltpu.VMEM((2, A_Q_HEADS, BLOCK), F32),
            pltpu.VMEM((A_Q_HEADS, BLOCK), F32),
            pltpu.VMEM((A_Q_HEADS, BLOCK), F32),
            pltpu.VMEM((A_Q_HEADS, VROWS, BLOCK), F32),
        ],
        compiler_params=_cparams("parallel", "arbitrary"),
        name="dsa_attn",
    )(q, k, vt, qi, ki, wi, cosT, sinT, qgain)


def _mlp_kernel(h_ref, o_ref, wo_ref, g_ref, w1_ref, w2_ref, out_ref):
    h1 = h_ref[...] + _dot(o_ref[...], wo_ref[...])
    hn = _rms_rows(h1, g_ref[...]).astype(BF16)
    acc = h1
    for c in range(D_FF // FF_CHUNK):
        u = jnp.maximum(_dot(hn, w1_ref[:, FF_CHUNK * c:FF_CHUNK * (c + 1)]), 0.0)
        acc = acc + _dot((u * u).astype(BF16), w2_ref[FF_CHUNK * c:FF_CHUNK * (c + 1), :])
    out_ref[...] = acc


def _out_mlp(h2, o2, wo, g, w1, w2, tm):
    n, d = h2.shape
    const = lambda i: (0, 0)
    tile = lambda i: (i, 0)
    once = pl.Buffered(1)
    return pl.pallas_call(
        _mlp_kernel,
        grid=(n // tm,),
        in_specs=[
            pl.BlockSpec((tm, d), tile),
            pl.BlockSpec((tm, d), tile),
            pl.BlockSpec((d, d), const, pipeline_mode=once),
            pl.BlockSpec((1, d), const),
            pl.BlockSpec((d, D_FF), const, pipeline_mode=once),
            pl.BlockSpec((D_FF, d), const, pipeline_mode=once),
        ],
        out_specs=pl.BlockSpec((tm, d), tile),
        out_shape=jax.ShapeDtypeStruct((n, d), F32),
        compiler_params=_cparams("parallel"),
        name="out_mlp",
    )(h2, o2, wo, g, w1, w2)


def _shared_kv_kernel(h_ref, g_ref, w_ref, e_ref, et_ref, kg_ref, fb_ref, k_o, v_o, lf_o):
    hn = _rms_rows(h_ref[0], g_ref[...]).astype(BF16)
    y = _dot(hn, w_ref[...])
    k_o[0] = _head_norm_nat(y[:, 0:1024], e_ref, et_ref, kg_ref[...]).astype(BF16)
    v_o[0] = y[:, 1024:2048].astype(BF16)
    x = y[:, 2048:2176] + fb_ref[...]
    lf_o[0] = jnp.minimum(x, 0.0) - jnp.log1p(jnp.exp(-jnp.abs(x)))


def _shared_kv(h, g, w_all, e, et, kgain, fbias, tm):
    b, tp, d = h.shape
    const = lambda *_: (0, 0)
    tile = lambda bi, j: (bi, j, 0)
    return pl.pallas_call(
        _shared_kv_kernel,
        grid=(b, tp // tm),
        in_specs=[
            pl.BlockSpec((1, tm, d), tile),
            pl.BlockSpec((1, d), const),
            pl.BlockSpec(w_all.shape, const),
            pl.BlockSpec(e.shape, const),
            pl.BlockSpec(et.shape, const),
            pl.BlockSpec((1, 1024), const),
            pl.BlockSpec((1, LANES), const),
        ],
        out_specs=[
            pl.BlockSpec((1, tm, 1024), tile),
            pl.BlockSpec((1, tm, 1024), tile),
            pl.BlockSpec((1, tm, LANES), tile),
        ],
        out_shape=[
            jax.ShapeDtypeStruct((b, tp, 1024), BF16),
            jax.ShapeDtypeStruct((b, tp, 1024), BF16),
            jax.ShapeDtypeStruct((b, tp, LANES), F32),
        ],
        compiler_params=_cparams("parallel", "parallel"),
        name="shared_kv",
    )(h, g, w_all, e, et, kgain, fbias)


def _cumsum_kernel(lf_ref, c_ref):
    r = lax.broadcasted_iota(jnp.int32, (BLOCK, BLOCK), 0)
    cidx = lax.broadcasted_iota(jnp.int32, (BLOCK, BLOCK), 1)
    tri = jnp.where(cidx <= r, 1.0, 0.0).astype(BF16)
    ntile = lf_ref.shape[1] // BLOCK

    def body(i, carry):
        off = pl.multiple_of(i * BLOCK, BLOCK)
        hi, mid, lo = _split3(lf_ref[0, pl.ds(off, BLOCK), :])
        cs = _dot(tri, hi) + _dot(tri, mid) + _dot(tri, lo) + carry
        c_ref[0, pl.ds(off, BLOCK), :] = cs
        return cs[BLOCK - 1:BLOCK, :]

    lax.fori_loop(0, ntile, body, jnp.zeros((1, LANES), F32))


def _cumsum_t(lf):
    b, tp, w = lf.shape
    spec = pl.BlockSpec((1, tp, w), lambda bi: (bi, 0, 0))
    return pl.pallas_call(
        _cumsum_kernel,
        grid=(b,),
        in_specs=[spec],
        out_specs=spec,
        out_shape=jax.ShapeDtypeStruct((b, tp, w), F32),
        compiler_params=_cparams("parallel"),
        name="cumsum_t",
    )(lf)


def _proj_q_kernel(h_ref, g_ref, w_ref, q_o):
    hn = _rms_rows(h_ref[...], g_ref[...]).astype(BF16)
    q_o[...] = _dot(hn, w_ref[...]).astype(BF16)


def _proj_q(h2, g, w, tm):
    n, d = h2.shape
    const = lambda i: (0, 0)
    tile = lambda i: (i, 0)
    return pl.pallas_call(
        _proj_q_kernel,
        grid=(n // tm,),
        in_specs=[pl.BlockSpec((tm, d), tile), pl.BlockSpec((1, d), const), pl.BlockSpec((d, d), const)],
        out_specs=pl.BlockSpec((tm, d), tile),
        out_shape=jax.ShapeDtypeStruct((n, d), BF16),
        compiler_params=_cparams("parallel"),
        name="proj_q",
    )(h2, g, w)


def _fox_layout_kernel(tp, k_ref, v_ref, c_ref, pk_ref, pc_ref, ka_o, vt_o):
    ci = pl.program_id(1)
    k = jnp.where(_valid_rows(ci, tp, k_ref.shape[1:]), k_ref[0], 0)
    cl = jnp.where(_valid_rows(ci, tp, c_ref.shape[1:]), c_ref[0] * LOG2E, 0.0)
    hi, mid, lo = _split3(cl)
    ka = _dot(k, pk_ref[...]) + _dot(hi, pc_ref[0]) + _dot(mid, pc_ref[1]) + _dot(lo, pc_ref[2])
    ka_o[0] = ka.astype(BF16)
    v = jnp.where(_valid_rows(ci, tp, v_ref.shape[1:]), v_ref[0].astype(F32), 0.0)
    vt_o[0, 0] = _values_T(v, B_HEADS).astype(BF16)


def _fox_layout(k, v, c, pk, pc, tk):
    b, tp, _ = k.shape
    nch = tk // CHUNK
    chunk = lambda bi, ci: (bi, ci, 0)
    return pl.pallas_call(
        functools.partial(_fox_layout_kernel, tp),
        grid=(b, nch),
        in_specs=[
            pl.BlockSpec((1, CHUNK, 1024), chunk),
            pl.BlockSpec((1, CHUNK, 1024), chunk),
            pl.BlockSpec((1, CHUNK, LANES), chunk),
            pl.BlockSpec(pk.shape, lambda bi, ci: (0, 0)),
            pl.BlockSpec(pc.shape, lambda bi, ci: (0, 0, 0)),
        ],
        out_specs=[
            pl.BlockSpec((1, CHUNK, B_HEADS * LANES), chunk),
            pl.BlockSpec((1, 1, B_HEADS * VROWS, CHUNK), lambda bi, ci: (bi, ci, 0, 0)),
        ],
        out_shape=[
            jax.ShapeDtypeStruct((b, tk, B_HEADS * LANES), BF16),
            jax.ShapeDtypeStruct((b, nch, B_HEADS * VROWS, CHUNK), BF16),
        ],
        compiler_params=_cparams("parallel", "parallel"),
        name="fox_layout",
    )(k, v, c, pk, pc)


def _fox_kernel(q_ref, ka_ref, vt_ref, qg_ref, o_ref, qx_ref, s_ref, p_ref, cm_ref, m_ref, al_ref, acc_ref):
    blk = pl.program_id(1)
    nch = (blk + 2) // 2
    qT = q_ref[0].astype(F32).T
    qscale = HEAD_DIM ** -0.5 * LOG2E
    row = lax.broadcasted_iota(jnp.int32, (HEAD_DIM, BLOCK), 0)
    minus3 = jnp.where(row < 3, -1.0, 0.0).astype(BF16)
    zeros = jnp.zeros((LANES, BLOCK), BF16)
    for pr in range(B_HEADS // 2):
        xa = _head_T(qT, 2 * pr, None, None, qg_ref[...], qscale).astype(BF16)
        xb = _head_T(qT, 2 * pr + 1, None, None, qg_ref[...], qscale).astype(BF16)
        left = jnp.concatenate([xa, minus3, zeros], axis=0)
        right = jnp.concatenate([zeros, xb, minus3], axis=0)
        qx_ref[pr] = jnp.concatenate([left, right], axis=1)

    _init_softmax(m_ref, acc_ref)
    qidx = blk * BLOCK + lax.broadcasted_iota(jnp.int32, (CHUNK, BLOCK), 1)
    krow = lax.broadcasted_iota(jnp.int32, (CHUNK, BLOCK), 0)

    npair = B_HEADS // 2

    def scores(c, pr, s_dst, cm_dst):
        off = pl.multiple_of(jnp.minimum(c, nch - 1) * CHUNK, CHUNK)
        bias = jnp.where(krow + off <= qidx, 0.0, NEG)
        sT = _dot(ka_ref[0, pl.ds(off, CHUNK), 2 * LANES * pr:2 * LANES * (pr + 1)], qx_ref[pr])
        for j in range(2):
            h = 2 * pr + j
            s = sT[:, BLOCK * j:BLOCK * (j + 1)] + bias
            s_dst[h] = s
            cm_dst[h:h + 1, :] = jnp.max(s, axis=0, keepdims=True)

    def probs(pr, s_src):
        for h in (2 * pr, 2 * pr + 1):
            p_ref[:, BLOCK * h:BLOCK * (h + 1)] = jnp.exp2(s_src[h] - m_ref[h:h + 1, :]).astype(BF16)

    def values(c):
        for pr in range(npair):
            pv = _dot(vt_ref[0, c, 2 * VROWS * pr:2 * VROWS * (pr + 1), :],
                      p_ref[:, 2 * BLOCK * pr:2 * BLOCK * (pr + 1)])
            for j in range(2):
                h = 2 * pr + j
                acc_ref[h] = (al_ref[h:h + 1, :] * acc_ref[h]
                              + pv[VROWS * j:VROWS * (j + 1), BLOCK * j:BLOCK * (j + 1)])

    def step(c, src, dst):
        _update_max(m_ref, src[1], al_ref)
        for pr in range(npair):
            if dst is not None:
                scores(c + 1, pr, *dst)
            probs(pr, src[0])
        values(c)

    buf_a = (s_ref.at[0], cm_ref.at[0])
    buf_b = (s_ref.at[1], cm_ref.at[1])
    for pr in range(npair):
        scores(0, pr, *buf_a)

    def pair_body(i, carry):
        step(2 * i, buf_a, buf_b)
        step(2 * i + 1, buf_b, buf_a)
        return carry

    lax.fori_loop(0, nch // 2, pair_body, 0)

    @pl.when(nch % 2 == 1)
    def _():
        step(nch - 1, buf_a, None)

    _finish_heads(B_HEADS, acc_ref, o_ref)


def _fox_attention(q, ka, vt, qgain):
    b, tp, _ = q.shape
    tk = ka.shape[1]
    qblk = lambda bi, j: (bi, j, 0)
    return pl.pallas_call(
        _fox_kernel,
        grid=(b, tp // BLOCK),
        in_specs=[
            pl.BlockSpec((1, BLOCK, 1024), qblk),
            pl.BlockSpec((1, tk, B_HEADS * LANES), lambda bi, j: (bi, 0, 0)),
            pl.BlockSpec((1,) + vt.shape[1:], lambda bi, j: (bi, 0, 0, 0)),
            pl.BlockSpec((HEAD_DIM, 1), lambda bi, j: (0, 0)),
        ],
        out_specs=pl.BlockSpec((1, BLOCK, 1024), qblk),
        out_shape=jax.ShapeDtypeStruct((b, tp, 1024), BF16),
        scratch_shapes=[
            pltpu.VMEM((B_HEADS // 2, 2 * LANES, 2 * BLOCK), BF16),
            pltpu.VMEM((2, B_HEADS, CHUNK, BLOCK), F32),
            pltpu.VMEM((CHUNK, B_HEADS * BLOCK), BF16),
            pltpu.VMEM((2, B_HEADS, BLOCK), F32),
            pltpu.VMEM((B_HEADS, BLOCK), F32),
            pltpu.VMEM((B_HEADS, BLOCK), F32),
            pltpu.VMEM((B_HEADS, VROWS, BLOCK), F32),
        ],
        compiler_params=_cparams("parallel", "arbitrary"),
        name="fox_attn",
    )(q, ka, vt, qgain)


def _indicator(nheads, width):
    e = (jnp.arange(nheads * HEAD_DIM)[:, None] // HEAD_DIM == jnp.arange(width)[None, :])
    return e.astype(BF16), e.T.astype(BF16)


def _pad_cols(w, n):
    return jnp.pad(w, ((0, 0), (0, n - w.shape[1])))


def _placements():
    src = jnp.arange(B_HEADS * HEAD_DIM)
    dst = jnp.arange(B_HEADS * LANES)
    pk = (dst[None, :] == (src[:, None] // HEAD_DIM) * LANES + src[:, None] % HEAD_DIM).astype(BF16)
    head = jnp.arange(LANES)
    pc = jnp.stack([(dst[None, :] == head[:, None] * LANES + HEAD_DIM + i) & (head[:, None] < B_HEADS)
                    for i in range(3)]).astype(BF16)
    return pk, pc


def kernel(x, meta_tokens, attn_norm, mlp_norm, mlp_w1, mlp_w2, a_w_in, a_q_gain, a_k_gain, a_w_out,
           kv_norm, kv_w, kv_f_bias, kv_k_gain, b_w_q, b_q_gain, b_w_out):
    b, s, d = x.shape
    t = s + N_META
    tp = -(-t // BLOCK) * BLOCK
    tk = -(-tp // CHUNK) * CHUNK
    tm = tp // 4
    depth = attn_norm.shape[0]
    n_a = a_w_in.shape[0]
    assert d == D_MODEL and tm % 16 == 0 and min(TOPK, s // 4) == TOPK and tk <= 2 ** IDX_BITS

    meta = jnp.broadcast_to(meta_tokens.astype(x.dtype)[None], (b, N_META, d))
    h = jnp.pad(jnp.concatenate([meta, x], axis=1), ((0, 0), (0, tp - t), (0, 0)))

    inv = 1.0 / (ROPE_THETA ** (jnp.arange(0, HEAD_DIM, 2, dtype=F32) / HEAD_DIM))
    ang = jnp.arange(tp, dtype=F32)[:, None] * inv[None, :]
    cos, sin = jnp.cos(ang), jnp.sin(ang)
    c2 = jnp.concatenate([cos, cos, cos, cos], axis=1)
    s2 = jnp.concatenate([-sin, sin, -sin, sin], axis=1)
    cosT, sinT = cos.T, sin.T

    e4, et4 = _indicator(A_KV_HEADS, LANES)
    e16, et16 = _indicator(B_HEADS, LANES)

    shared = None
    for i in range(depth):
        g_attn = attn_norm[i][None, :]
        if i < n_a:
            w_all = _pad_cols(a_w_in[i], 2176).astype(BF16)
            kgain = jnp.tile(a_k_gain[i], A_KV_HEADS)[None, :]
            q, k, v, qi, ki, wi = _proj_a(h, g_attn, w_all, e4, et4, kgain, c2, s2, tm)
            kp, vt, kip = _dsa_layout(k, v, ki, tk)
            o = _dsa_attention(q, kp, vt, qi, kip, wi, cosT, sinT, a_q_gain[i][:, None])
            wo = a_w_out[i]
        else:
            if shared is None:
                w_all = _pad_cols(kv_w, 2176).astype(BF16)
                kgain = jnp.tile(kv_k_gain, B_HEADS)[None, :]
                fbias = jnp.pad(kv_f_bias, (0, LANES - B_HEADS))[None, :]
                k, v, lf = _shared_kv(h, kv_norm[None, :], w_all, e16, et16, kgain, fbias, tm)
                pk, pc = _placements()
                shared = _fox_layout(k, v, _cumsum_t(lf), pk, pc, tk)
            j = i - n_a
            q = _proj_q(h.reshape(b * tp, d), g_attn, b_w_q[j].astype(BF16), tm).reshape(b, tp, d)
            o = _fox_attention(q, shared[0], shared[1], b_q_gain[j][:, None])
            wo = b_w_out[j]
        h = _out_mlp(h.reshape(b * tp, d), o.reshape(b * tp, d), wo.astype(BF16), mlp_norm[i][None, :],
                     mlp_w1[i].astype(BF16), mlp_w2[i].astype(BF16), tm).reshape(b, tp, d)
    return h[:, N_META:N_META + s]
```

```python
import functools
import math

import jax
import jax.numpy as jnp
from jax import lax
from jax.experimental import pallas as pl
from jax.experimental.pallas import tpu as pltpu

F32 = jnp.float32
BF16 = jnp.bfloat16

D_MODEL = 1024
HEAD_DIM = 64
HALF = HEAD_DIM // 2
N_META = 16
BLOCK = 128
CHUNK = 256
PASS_ROWS = 2 * CHUNK
FOLD = 64
A_Q_HEADS = 16
A_KV_HEADS = 4
A_REP = A_Q_HEADS // A_KV_HEADS
IDX_HEADS = 8
B_HEADS = 16
D_FF = 4 * D_MODEL
FF_CHUNK = 1024
TOPK = 256
EPS = 1e-6
ROPE_THETA = 10000.0
LOG2E = math.log2(math.e)
NEG = -1e30
N_BISECT = 20
IDX_BITS = 12
TIE_WALK_MAX = 10
VROWS = 80
LANES = 128
VMEM_LIMIT = 52 * 1024 * 1024


def _cparams(*sem):
    return pltpu.CompilerParams(dimension_semantics=sem, vmem_limit_bytes=VMEM_LIMIT)


def _dot(a, b):
    return jnp.dot(a, b, preferred_element_type=F32)


def _split2(x):
    hi = x.astype(BF16)
    lo = (x - hi.astype(F32)).astype(BF16)
    return hi, lo


def _split3(x):
    hi = x.astype(BF16)
    r = x - hi.astype(F32)
    mid = r.astype(BF16)
    lo = (r - mid.astype(F32)).astype(BF16)
    return hi, mid, lo


def _rms_rows(h, g):
    ms = jnp.mean(h * h, axis=-1, keepdims=True)
    return h * lax.rsqrt(ms + EPS) * g


def _head_norm_nat(x, e_ref, et_ref, gain):
    x2 = x * x
    hi, lo = _split2(x2)
    ss = _dot(hi, e_ref[...]) + _dot(lo, e_ref[...])
    inv = lax.rsqrt(ss * (1.0 / HEAD_DIM) + EPS)
    ihi, ilo = _split2(inv)
    invx = _dot(ihi, et_ref[...]) + _dot(ilo, et_ref[...])
    return x * invx * gain


def _rope_nat(x, c2, s2):
    w = x.shape[1]
    reps = w // LANES
    c = c2 if reps == 1 else jnp.concatenate([c2] * reps, axis=1)
    s = s2 if reps == 1 else jnp.concatenate([s2] * reps, axis=1)
    lane = lax.broadcasted_iota(jnp.int32, x.shape, 1)
    first = (lane & HALF) == 0
    partner = jnp.where(first, pltpu.roll(x, w - HALF, 1), pltpu.roll(x, HALF, 1))
    return x * c + partner * s


def _proj_a_kernel(h_ref, g_ref, w_ref, e_ref, et_ref, kg_ref, c2_ref, s2_ref,
                   q_o, k_o, v_o, qi_o, ki_o, wi_o):
    hn = _rms_rows(h_ref[0], g_ref[...]).astype(BF16)
    y = _dot(hn, w_ref[...])
    q_o[0] = y[:, 0:1024].astype(BF16)
    c2 = c2_ref[...]
    s2 = s2_ref[...]
    k = _head_norm_nat(y[:, 1024:1280], e_ref, et_ref, kg_ref[...])
    k_o[0] = _rope_nat(k, c2, s2).astype(BF16)
    v_o[0] = y[:, 1280:1536].astype(BF16)
    qi_o[0] = y[:, 1536:2048].astype(BF16)
    kiw = y[:, 2048:2176]
    ki_o[0] = _rope_nat(kiw, c2, s2)[:, 0:HEAD_DIM].astype(BF16)
    wi_o[0] = kiw * (IDX_HEADS ** -0.5 * HEAD_DIM ** -0.5)


def _proj_a(h, g, w_all, e, et, kgain, c2, s2, tm):
    b, tp, d = h.shape
    nt = tp // tm
    wn = w_all.shape[1]
    const = lambda *_: (0, 0)
    tile = lambda bi, j: (bi, j, 0)
    return pl.pallas_call(
        _proj_a_kernel,
        grid=(b, nt),
        in_specs=[
            pl.BlockSpec((1, tm, d), tile),
            pl.BlockSpec((1, d), const),
            pl.BlockSpec((d, wn), const),
            pl.BlockSpec(e.shape, const),
            pl.BlockSpec(et.shape, const),
            pl.BlockSpec((1, 256), const),
            pl.BlockSpec((tm, LANES), lambda bi, j: (j, 0)),
            pl.BlockSpec((tm, LANES), lambda bi, j: (j, 0)),
        ],
        out_specs=[
            pl.BlockSpec((1, tm, 1024), tile),
            pl.BlockSpec((1, tm, 256), tile),
            pl.BlockSpec((1, tm, 256), tile),
            pl.BlockSpec((1, tm, 512), tile),
            pl.BlockSpec((1, tm, HEAD_DIM), tile),
            pl.BlockSpec((1, tm, LANES), tile),
        ],
        out_shape=[
            jax.ShapeDtypeStruct((b, tp, 1024), BF16),
            jax.ShapeDtypeStruct((b, tp, 256), BF16),
            jax.ShapeDtypeStruct((b, tp, 256), BF16),
            jax.ShapeDtypeStruct((b, tp, 512), BF16),
            jax.ShapeDtypeStruct((b, tp, HEAD_DIM), BF16),
            jax.ShapeDtypeStruct((b, tp, LANES), F32),
        ],
        compiler_params=_cparams("parallel", "parallel"),
        name="proj_a",
    )(h, g, w_all, e, et, kgain, c2, s2)


def _head_T(xT, h, cosT, sinT, gain, scale):
    x = xT[HEAD_DIM * h:HEAD_DIM * (h + 1)]
    if gain is not None:
        ss = jnp.sum(x * x, axis=0, keepdims=True)
        x = x * lax.rsqrt(ss * (1.0 / HEAD_DIM) + EPS) * gain
    if cosT is not None:
        x1 = x[:HALF]
        x2 = x[HALF:]
        x = jnp.concatenate([x1 * cosT - x2 * sinT, x1 * sinT + x2 * cosT], axis=0)
    if scale is not None:
        x = x * scale
    return x


def _update_max(m_ref, cm_ref, al_ref):
    m_old = m_ref[...]
    m_new = jnp.maximum(m_old, cm_ref[...])
    al_ref[...] = jnp.exp2(m_old - m_new)
    m_ref[...] = m_new


def _init_softmax(m_ref, acc_ref):
    m_ref[...] = jnp.full(m_ref.shape, NEG, F32)
    acc_ref[...] = jnp.zeros(acc_ref.shape, F32)


def _finish_heads(nheads, acc_ref, o_ref):
    outs = [acc_ref[h, 0:HEAD_DIM, :] / acc_ref[h, HEAD_DIM:HEAD_DIM + 1, :] for h in range(nheads)]
    oT = jnp.concatenate(outs, axis=0)
    o_ref[0] = oT.T.astype(BF16)


def _values_T(v, nheads):
    vT = v.T
    row = lax.broadcasted_iota(jnp.int32, (VROWS - HEAD_DIM, v.shape[0]), 0)
    ones = jnp.where(row == 0, 1.0, 0.0)
    parts = []
    for h in range(nheads):
        parts += [vT[HEAD_DIM * h:HEAD_DIM * (h + 1)], ones]
    return jnp.concatenate(parts, axis=0)


def _valid_rows(c, tp, shape):
    return c * CHUNK + lax.broadcasted_iota(jnp.int32, shape, 0) < tp


def _dsa_kernel(q_ref, k_ref, vt_ref, qi_ref, ki_ref, wi_ref, cos_ref, sin_ref, qg_ref,
                o_ref, sc_ref, tie_ref, qx_ref, qix_ref, thr_ref, cut_ref, bias_ref, s_ref, p_ref, cm_ref, m_ref,
                al_ref, acc_ref):
    blk = pl.program_id(1)
    nch = (blk + 2) // 2
    cosT = cos_ref[...]
    sinT = sin_ref[...]

    qT = q_ref[0].astype(F32).T
    qscale = HEAD_DIM ** -0.5 * LOG2E
    @pl.when(blk == 0)
    def _():
        qx_ref[...] = jnp.zeros(qx_ref.shape, BF16)

    for g in range(A_KV_HEADS):
        heads = [_head_T(qT, A_REP * g + j, cosT, sinT, qg_ref[...], qscale).astype(BF16)
                 for j in range(A_REP)]
        qx_ref[g, HEAD_DIM * g:HEAD_DIM * (g + 1), :] = jnp.concatenate(heads, axis=1)

    qiT = qi_ref[0].astype(F32).T
    qix_ref[...] = jnp.concatenate(
        [_head_T(qiT, h, cosT, sinT, None, None).astype(BF16) for h in range(IDX_HEADS)], axis=1)
    wT = wi_ref[0].T

    qidx = blk * BLOCK + lax.broadcasted_iota(jnp.int32, (CHUNK, BLOCK), 1)
    krow = lax.broadcasted_iota(jnp.int32, (CHUNK, BLOCK), 0)

    def idx_body(c, carry):
        mx, mn = carry
        off = pl.multiple_of(c * CHUNK, CHUNK)
        lg = _dot(ki_ref[0, pl.ds(off, CHUNK), :], qix_ref[...])
        acc = jnp.zeros((CHUNK, BLOCK), F32)
        for h in range(IDX_HEADS):
            w_h = wT[HEAD_DIM + h:HEAD_DIM + h + 1, :]
            acc = acc + w_h * jnp.maximum(lg[:, BLOCK * h:BLOCK * (h + 1)], 0.0)
        causal = krow + off <= qidx
        hi_part = jnp.where(causal, acc, -jnp.inf)
        lo_part = jnp.where(causal, acc, jnp.inf)
        sc_ref[pl.ds(off, CHUNK), :] = hi_part
        mx = jnp.maximum(mx, jnp.max(hi_part.reshape(CHUNK // FOLD, FOLD, BLOCK), axis=0))
        mn = jnp.minimum(mn, jnp.min(lo_part.reshape(CHUNK // FOLD, FOLD, BLOCK), axis=0))
        return mx, mn

    mx, mn = lax.fori_loop(0, nch, idx_body, (jnp.full((FOLD, BLOCK), -jnp.inf, F32),
                                              jnp.full((FOLD, BLOCK), jnp.inf, F32)))

    @pl.when(nch % 2 == 1)
    def _():
        sc_ref[pl.ds(pl.multiple_of(nch * CHUNK, CHUNK), CHUNK), :] = jnp.full((CHUNK, BLOCK), -jnp.inf, F32)

    thr_ref[...] = jnp.full((1, BLOCK), -jnp.inf, F32)
    cut_ref[...] = jnp.full((1, BLOCK), -1, jnp.int32)

    npass = (nch + 1) // 2
    prow = lax.broadcasted_iota(jnp.int32, (PASS_ROWS, BLOCK), 0)

    def sweep(fn, inits, combine, finish, src=sc_ref):
        def body(c, accs):
            off = pl.multiple_of(c * PASS_ROWS, PASS_ROWS)
            vals = fn(src[pl.ds(off, PASS_ROWS), :], off)
            return tuple(combine(a, v.reshape(PASS_ROWS // FOLD, FOLD, BLOCK)) for a, v in zip(accs, vals))
        return tuple(finish(a) for a in lax.fori_loop(0, npass, body, inits))

    def counts(*preds, src=sc_ref):
        return sweep(lambda s, off: tuple(jnp.where(p(s, off), 1.0, 0.0) for p in preds),
                     (jnp.zeros((FOLD, BLOCK), F32),) * len(preds),
                     lambda a, v: a + jnp.sum(v, axis=0),
                     lambda a: jnp.sum(a, axis=0, keepdims=True), src)

    def min_above(t):
        return sweep(lambda s, off: (jnp.where(s > t, s, jnp.inf),),
                     (jnp.full((FOLD, BLOCK), jnp.inf, F32),),
                     lambda a, v: jnp.minimum(a, jnp.min(v, axis=0)),
                     lambda a: jnp.min(a, axis=0, keepdims=True))[0]

    kf = float(TOPK)

    @pl.when(blk >= 2)
    def _():
        vmax = jnp.max(mx, axis=0, keepdims=True)
        vmin = jnp.min(mn, axis=0, keepdims=True)
        below = vmin - (jnp.abs(vmin) * 1e-3 + 1e-30)

        def bis(_, lohi):
            lo, hi = lohi
            mid = 0.5 * (lo + hi)
            ok = counts(lambda s, off: s > mid)[0] >= kf
            return jnp.where(ok, mid, lo), jnp.where(ok, hi, mid)

        lo, hi = lax.fori_loop(0, N_BISECT, bis, (below, vmax))

        def probe(lo):
            cand = min_above(lo)
            return (cand,) + counts(lambda s, off: s > cand, lambda s, off: s >= cand)

        def walk_cond(st):
            return jnp.max(jnp.where(st[2] >= kf, 1.0, 0.0)) > 0.0

        def walk_body(st):
            lo, cand, gt, _ = st
            lo = jnp.where(gt >= kf, cand, lo)
            return (lo,) + probe(lo)

        _, thr, cnt_gt, cnt_ge = lax.while_loop(walk_cond, walk_body, (lo,) + probe(lo))
        thr_ref[...] = thr
        cut_ref[...] = jnp.full((1, BLOCK), 2 ** IDX_BITS, jnp.int32)

        excess = (cnt_ge - kf).astype(jnp.int32)
        most = jnp.max(excess)

        def mark_ties():
            def mark(c, carry):
                off = pl.multiple_of(c * PASS_ROWS, PASS_ROWS)
                tied = sc_ref[pl.ds(off, PASS_ROWS), :] == thr
                tie_ref[pl.ds(off, PASS_ROWS), :] = jnp.where(tied, prow + off, 2 ** IDX_BITS)
                return carry

            lax.fori_loop(0, npass, mark, 0)

        @pl.when((most > 0) & (most <= TIE_WALK_MAX))
        def _():
            mark_ties()

            def drop(_, st):
                cut, rem = st
                top = sweep(lambda e, off: (jnp.where(e <= cut, e, -1),),
                            (jnp.full((FOLD, BLOCK), -1, jnp.int32),),
                            lambda a, v: jnp.maximum(a, jnp.max(v, axis=0)),
                            lambda a: jnp.max(a, axis=0, keepdims=True), tie_ref)[0]
                return jnp.where(rem > 0, top - 1, cut), rem - 1

            every = jnp.full((1, BLOCK), 2 ** IDX_BITS - 1, jnp.int32)
            cut_ref[...] = lax.fori_loop(0, most, drop, (every, excess))[0]

        @pl.when(most > TIE_WALK_MAX)
        def _():
            need = kf - cnt_gt
            mark_ties()

            grow = jnp.where(npass > 1, 1, 0) + jnp.where(npass > 2, 1, 0) + jnp.where(npass > 4, 1, 0)
            half0 = lax.shift_left(jnp.int32(PASS_ROWS // 2), grow)

            def tie_bis(i, lo_i):
                mid = lo_i + lax.shift_right_logical(half0, i)
                got = counts(lambda e, off: e <= mid, src=tie_ref)[0]
                return jnp.where(got >= need, lo_i, mid)

            nsteps = (PASS_ROWS.bit_length() - 1) + grow
            lo_i = lax.fori_loop(0, nsteps, tie_bis, jnp.full((1, BLOCK), -1, jnp.int32))
            cut_ref[...] = lo_i + 1

    _init_softmax(m_ref, acc_ref)

    def chunk_off(c):
        return pl.multiple_of(jnp.minimum(c, nch - 1) * CHUNK, CHUNK)

    def set_bias(c):
        off = chunk_off(c)
        s_idx = sc_ref[pl.ds(off, CHUNK), :]
        thr = thr_ref[...]
        sel = (s_idx > thr) | ((s_idx == thr) & (krow + off <= cut_ref[...]))
        bias_ref[...] = jnp.where(sel, 0.0, NEG)

    def scores(c, g, s_dst, cm_dst):
        sT = _dot(k_ref[0, pl.ds(chunk_off(c), CHUNK), :], qx_ref[g])
        bias = bias_ref[...]
        s = sT + jnp.concatenate([bias] * A_REP, axis=1)
        cols = slice(A_REP * BLOCK * g, A_REP * BLOCK * (g + 1))
        s_dst[:, cols] = s
        cm_dst[:, cols] = jnp.max(s, axis=0, keepdims=True)

    def probs(g, s_src):
        cols = slice(A_REP * BLOCK * g, A_REP * BLOCK * (g + 1))
        p_ref[:, cols] = jnp.exp2(s_src[:, cols] - m_ref[:, cols]).astype(BF16)

    def values(c, g):
        w = A_REP * BLOCK
        pv = _dot(vt_ref[0, c, VROWS * g:VROWS * (g + 1), :], p_ref[:, w * g:w * (g + 1)])
        for j in range(A_REP):
            h = A_REP * g + j
            acc_ref[h] = al_ref[:, BLOCK * h:BLOCK * (h + 1)] * acc_ref[h] + pv[:, BLOCK * j:BLOCK * (j + 1)]

    def step(c, src, dst):
        _update_max(m_ref, src[1], al_ref)
        if dst is not None:
            set_bias(c + 1)
        for g in range(A_KV_HEADS):
            if dst is not None:
                scores(c + 1, g, *dst)
            probs(g, src[0])
        for g in range(A_KV_HEADS):
            values(c, g)

    buf_a = (s_ref.at[0], cm_ref.at[0])
    buf_b = (s_ref.at[1], cm_ref.at[1])
    set_bias(0)
    for g in range(A_KV_HEADS):
        scores(0, g, *buf_a)

    def pair_body(i, carry):
        step(2 * i, buf_a, buf_b)
        step(2 * i + 1, buf_b, buf_a)
        return carry

    lax.fori_loop(0, nch // 2, pair_body, 0)

    @pl.when(nch % 2 == 1)
    def _():
        step(nch - 1, buf_a, None)

    _finish_heads(A_Q_HEADS, acc_ref, o_ref)


def _dsa_layout_kernel(tp, k_ref, v_ref, ki_ref, k_o, vt_o, ki_o):
    c = pl.program_id(1)
    k_o[0] = jnp.where(_valid_rows(c, tp, k_ref.shape[1:]), k_ref[0], 0)
    ki_o[0] = jnp.where(_valid_rows(c, tp, ki_ref.shape[1:]), ki_ref[0], 0)
    v = jnp.where(_valid_rows(c, tp, v_ref.shape[1:]), v_ref[0].astype(F32), 0.0)
    vt_o[0, 0] = _values_T(v, A_KV_HEADS).astype(BF16)


def _dsa_layout(k, v, ki, tk):
    b, tp, _ = k.shape
    nch = tk // CHUNK
    chunk = lambda bi, c: (bi, c, 0)
    return pl.pallas_call(
        functools.partial(_dsa_layout_kernel, tp),
        grid=(b, nch),
        in_specs=[
            pl.BlockSpec((1, CHUNK, 256), chunk),
            pl.BlockSpec((1, CHUNK, 256), chunk),
            pl.BlockSpec((1, CHUNK, HEAD_DIM), chunk),
        ],
        out_specs=[
            pl.BlockSpec((1, CHUNK, 256), chunk),
            pl.BlockSpec((1, 1, A_KV_HEADS * VROWS, CHUNK), lambda bi, c: (bi, c, 0, 0)),
            pl.BlockSpec((1, CHUNK, HEAD_DIM), chunk),
        ],
        out_shape=[
            jax.ShapeDtypeStruct((b, tk, 256), BF16),
            jax.ShapeDtypeStruct((b, nch, A_KV_HEADS * VROWS, CHUNK), BF16),
            jax.ShapeDtypeStruct((b, tk, HEAD_DIM), BF16),
        ],
        compiler_params=_cparams("parallel", "parallel"),
        name="dsa_layout",
    )(k, v, ki)


def _dsa_attention(q, k, vt, qi, ki, wi, cosT, sinT, qgain):
    b, tp, _ = q.shape
    tk = k.shape[1]
    nblk = tp // BLOCK
    qblk = lambda bi, j: (bi, j, 0)
    seq = lambda bi, j: (bi, 0, 0)
    return pl.pallas_call(
        _dsa_kernel,
        grid=(b, nblk),
        in_specs=[
            pl.BlockSpec((1, BLOCK, 1024), qblk),
            pl.BlockSpec((1, tk, 256), seq),
            pl.BlockSpec((1,) + vt.shape[1:], lambda bi, j: (bi, 0, 0, 0)),
            pl.BlockSpec((1, BLOCK, 512), qblk),
            pl.BlockSpec((1, tk, HEAD_DIM), seq),
            pl.BlockSpec((1, BLOCK, LANES), qblk),
            pl.BlockSpec((HALF, BLOCK), lambda bi, j: (0, j)),
            pl.BlockSpec((HALF, BLOCK), lambda bi, j: (0, j)),
            pl.BlockSpec((HEAD_DIM, 1), lambda bi, j: (0, 0)),
        ],
        out_specs=pl.BlockSpec((1, BLOCK, 1024), qblk),
        out_shape=jax.ShapeDtypeStruct((b, tp, 1024), BF16),
        scratch_shapes=[
            pltpu.VMEM((-(-tk // PASS_ROWS) * PASS_ROWS, BLOCK), F32),
            pltpu.VMEM((-(-tk // PASS_ROWS) * PASS_ROWS, BLOCK), jnp.int32),
            pltpu.VMEM((A_KV_HEADS, 256, A_REP * BLOCK), BF16),
            pltpu.VMEM((HEAD_DIM, IDX_HEADS * BLOCK), BF16),
            pltpu.VMEM((1, BLOCK), F32),
            pltpu.VMEM((1, BLOCK), jnp.int32),
            pltpu.VMEM((CHUNK, BLOCK), F32),
            pltpu.VMEM((2, CHUNK, A_Q_HEADS * BLOCK), F32),
            pltpu.VMEM((CHUNK, A_Q_HEADS * BLOCK), BF16),
            pltpu.VMEM((2, 1, A_Q_HEADS * BLOCK), F32),
            pltpu.VMEM((1, A_Q_HEADS * BLOCK), F32),
            pltpu.VMEM((1, A_Q_HEADS * BLOCK), F32),
            pltpu.VMEM((A_Q_HEADS, VROWS, BLOCK), F32),
        ],
        compiler_params=_cparams("parallel", "arbitrary"),
        name="dsa_attn",
    )(q, k, vt, qi, ki, wi, cosT, sinT, qgain)


def _mlp_kernel(h_ref, o_ref, wo_ref, g_ref, w1_ref, w2_ref, out_ref):
    h1 = h_ref[...] + _dot(o_ref[...], wo_ref[...])
    hn = _rms_rows(h1, g_ref[...]).astype(BF16)
    acc = h1
    for c in range(D_FF // FF_CHUNK):
        u = jnp.maximum(_dot(hn, w1_ref[:, FF_CHUNK * c:FF_CHUNK * (c + 1)]), 0.0)
        acc = acc + _dot((u * u).astype(BF16), w2_ref[FF_CHUNK * c:FF_CHUNK * (c + 1), :])
    out_ref[...] = acc


def _out_mlp(h2, o2, wo, g, w1, w2, tm):
    n, d = h2.shape
    const = lambda i: (0, 0)
    tile = lambda i: (i, 0)
    once = pl.Buffered(1)
    return pl.pallas_call(
        _mlp_kernel,
        grid=(n // tm,),
        in_specs=[
            pl.BlockSpec((tm, d), tile),
            pl.BlockSpec((tm, d), tile),
            pl.BlockSpec((d, d), const, pipeline_mode=once),
            pl.BlockSpec((1, d), const),
            pl.BlockSpec((d, D_FF), const, pipeline_mode=once),
            pl.BlockSpec((D_FF, d), const, pipeline_mode=once),
        ],
        out_specs=pl.BlockSpec((tm, d), tile),
        out_shape=jax.ShapeDtypeStruct((n, d), F32),
        compiler_params=_cparams("parallel"),
        name="out_mlp",
    )(h2, o2, wo, g, w1, w2)


def _shared_kv_kernel(h_ref, g_ref, w_ref, e_ref, et_ref, kg_ref, fb_ref, k_o, v_o, lf_o):
    hn = _rms_rows(h_ref[0], g_ref[...]).astype(BF16)
    y = _dot(hn, w_ref[...])
    k_o[0] = _head_norm_nat(y[:, 0:1024], e_ref, et_ref, kg_ref[...]).astype(BF16)
    v_o[0] = y[:, 1024:2048].astype(BF16)
    x = y[:, 2048:2176] + fb_ref[...]
    lf_o[0] = jnp.minimum(x, 0.0) - jnp.log1p(jnp.exp(-jnp.abs(x)))


def _shared_kv(h, g, w_all, e, et, kgain, fbias, tm):
    b, tp, d = h.shape
    const = lambda *_: (0, 0)
    tile = lambda bi, j: (bi, j, 0)
    return pl.pallas_call(
        _shared_kv_kernel,
        grid=(b, tp // tm),
        in_specs=[
            pl.BlockSpec((1, tm, d), tile),
            pl.BlockSpec((1, d), const),
            pl.BlockSpec(w_all.shape, const),
            pl.BlockSpec(e.shape, const),
            pl.BlockSpec(et.shape, const),
            pl.BlockSpec((1, 1024), const),
            pl.BlockSpec((1, LANES), const),
        ],
        out_specs=[
            pl.BlockSpec((1, tm, 1024), tile),
            pl.BlockSpec((1, tm, 1024), tile),
            pl.BlockSpec((1, tm, LANES), tile),
        ],
        out_shape=[
            jax.ShapeDtypeStruct((b, tp, 1024), BF16),
            jax.ShapeDtypeStruct((b, tp, 1024), BF16),
            jax.ShapeDtypeStruct((b, tp, LANES), F32),
        ],
        compiler_params=_cparams("parallel", "parallel"),
        name="shared_kv",
    )(h, g, w_all, e, et, kgain, fbias)


def _cumsum_kernel(lf_ref, c_ref):
    r = lax.broadcasted_iota(jnp.int32, (BLOCK, BLOCK), 0)
    cidx = lax.broadcasted_iota(jnp.int32, (BLOCK, BLOCK), 1)
    tri = jnp.where(cidx <= r, 1.0, 0.0).astype(BF16)
    ntile = lf_ref.shape[1] // BLOCK

    def body(i, carry):
        off = pl.multiple_of(i * BLOCK, BLOCK)
        hi, mid, lo = _split3(lf_ref[0, pl.ds(off, BLOCK), :])
        cs = _dot(tri, hi) + _dot(tri, mid) + _dot(tri, lo) + carry
        c_ref[0, pl.ds(off, BLOCK), :] = cs
        return cs[BLOCK - 1:BLOCK, :]

    lax.fori_loop(0, ntile, body, jnp.zeros((1, LANES), F32))


def _cumsum_t(lf):
    b, tp, w = lf.shape
    spec = pl.BlockSpec((1, tp, w), lambda bi: (bi, 0, 0))
    return pl.pallas_call(
        _cumsum_kernel,
        grid=(b,),
        in_specs=[spec],
        out_specs=spec,
        out_shape=jax.ShapeDtypeStruct((b, tp, w), F32),
        compiler_params=_cparams("parallel"),
        name="cumsum_t",
    )(lf)


def _proj_q_kernel(h_ref, g_ref, w_ref, q_o):
    hn = _rms_rows(h_ref[...], g_ref[...]).astype(BF16)
    q_o[...] = _dot(hn, w_ref[...]).astype(BF16)


def _proj_q(h2, g, w, tm):
    n, d = h2.shape
    const = lambda i: (0, 0)
    tile = lambda i: (i, 0)
    return pl.pallas_call(
        _proj_q_kernel,
        grid=(n // tm,),
        in_specs=[pl.BlockSpec((tm, d), tile), pl.BlockSpec((1, d), const), pl.BlockSpec((d, d), const)],
        out_specs=pl.BlockSpec((tm, d), tile),
        out_shape=jax.ShapeDtypeStruct((n, d), BF16),
        compiler_params=_cparams("parallel"),
        name="proj_q",
    )(h2, g, w)


def _fox_layout_kernel(tp, k_ref, v_ref, c_ref, pk_ref, pc_ref, ka_o, vt_o):
    ci = pl.program_id(1)
    k = jnp.where(_valid_rows(ci, tp, k_ref.shape[1:]), k_ref[0], 0)
    cl = jnp.where(_valid_rows(ci, tp, c_ref.shape[1:]), c_ref[0] * LOG2E, 0.0)
    hi, mid, lo = _split3(cl)
    ka = _dot(k, pk_ref[...]) + _dot(hi, pc_ref[0]) + _dot(mid, pc_ref[1]) + _dot(lo, pc_ref[2])
    ka_o[0] = ka.astype(BF16)
    v = jnp.where(_valid_rows(ci, tp, v_ref.shape[1:]), v_ref[0].astype(F32), 0.0)
    vt_o[0, 0] = _values_T(v, B_HEADS).astype(BF16)


def _fox_layout(k, v, c, pk, pc, tk):
    b, tp, _ = k.shape
    nch = tk // CHUNK
    chunk = lambda bi, ci: (bi, ci, 0)
    return pl.pallas_call(
        functools.partial(_fox_layout_kernel, tp),
        grid=(b, nch),
        in_specs=[
            pl.BlockSpec((1, CHUNK, 1024), chunk),
            pl.BlockSpec((1, CHUNK, 1024), chunk),
            pl.BlockSpec((1, CHUNK, LANES), chunk),
            pl.BlockSpec(pk.shape, lambda bi, ci: (0, 0)),
            pl.BlockSpec(pc.shape, lambda bi, ci: (0, 0, 0)),
        ],
        out_specs=[
            pl.BlockSpec((1, CHUNK, B_HEADS * LANES), chunk),
            pl.BlockSpec((1, 1, B_HEADS * VROWS, CHUNK), lambda bi, ci: (bi, ci, 0, 0)),
        ],
        out_shape=[
            jax.ShapeDtypeStruct((b, tk, B_HEADS * LANES), BF16),
            jax.ShapeDtypeStruct((b, nch, B_HEADS * VROWS, CHUNK), BF16),
        ],
        compiler_params=_cparams("parallel", "parallel"),
        name="fox_layout",
    )(k, v, c, pk, pc)


def _fox_kernel(q_ref, ka_ref, vt_ref, qg_ref, o_ref, qx_ref, s_ref, p_ref, cm_ref, m_ref, al_ref, acc_ref):
    blk = pl.program_id(1)
    nch = (blk + 2) // 2
    qT = q_ref[0].astype(F32).T
    qscale = HEAD_DIM ** -0.5 * LOG2E
    row = lax.broadcasted_iota(jnp.int32, (HEAD_DIM, BLOCK), 0)
    @pl.when(blk == 0)
    def _():
        minus3 = jnp.where(row < 3, -1.0, 0.0).astype(BF16)
        zeros = jnp.zeros((LANES, BLOCK), BF16)
        left = jnp.concatenate([zeros[:HEAD_DIM], minus3, zeros], axis=0)
        right = jnp.concatenate([zeros, zeros[:HEAD_DIM], minus3], axis=0)
        for pr in range(B_HEADS // 2):
            qx_ref[pr] = jnp.concatenate([left, right], axis=1)

    for pr in range(B_HEADS // 2):
        for j in range(2):
            x = _head_T(qT, 2 * pr + j, None, None, qg_ref[...], qscale).astype(BF16)
            qx_ref[pr, 2 * HEAD_DIM * j:2 * HEAD_DIM * j + HEAD_DIM, BLOCK * j:BLOCK * (j + 1)] = x

    _init_softmax(m_ref, acc_ref)
    qidx = blk * BLOCK + lax.broadcasted_iota(jnp.int32, (CHUNK, BLOCK), 1)
    krow = lax.broadcasted_iota(jnp.int32, (CHUNK, BLOCK), 0)

    npair = B_HEADS // 2

    def scores(c, pr, s_dst, cm_dst):
        off = pl.multiple_of(jnp.minimum(c, nch - 1) * CHUNK, CHUNK)
        bias = jnp.where(krow + off <= qidx, 0.0, NEG)
        sT = _dot(ka_ref[0, pl.ds(off, CHUNK), 2 * LANES * pr:2 * LANES * (pr + 1)], qx_ref[pr])
        s = sT + jnp.concatenate([bias, bias], axis=1)
        cols = slice(2 * BLOCK * pr, 2 * BLOCK * (pr + 1))
        s_dst[:, cols] = s
        cm_dst[:, cols] = jnp.max(s, axis=0, keepdims=True)

    def probs(pr, s_src):
        cols = slice(2 * BLOCK * pr, 2 * BLOCK * (pr + 1))
        p_ref[:, cols] = jnp.exp2(s_src[:, cols] - m_ref[:, cols]).astype(BF16)

    def values(c, pr):
        pv = _dot(vt_ref[0, c, 2 * VROWS * pr:2 * VROWS * (pr + 1), :],
                  p_ref[:, 2 * BLOCK * pr:2 * BLOCK * (pr + 1)])
        for j in range(2):
            h = 2 * pr + j
            acc_ref[h] = (al_ref[:, BLOCK * h:BLOCK * (h + 1)] * acc_ref[h]
                          + pv[VROWS * j:VROWS * (j + 1), BLOCK * j:BLOCK * (j + 1)])

    def step(c, src, dst):
        _update_max(m_ref, src[1], al_ref)
        for pr in range(npair):
            if dst is not None:
                scores(c + 1, pr, *dst)
            probs(pr, src[0])
            if pr > 0:
                values(c, pr - 1)
        values(c, npair - 1)

    buf_a = (s_ref.at[0], cm_ref.at[0])
    buf_b = (s_ref.at[1], cm_ref.at[1])
    for pr in range(npair):
        scores(0, pr, *buf_a)

    def pair_body(i, carry):
        step(2 * i, buf_a, buf_b)
        step(2 * i + 1, buf_b, buf_a)
        return carry

    lax.fori_loop(0, nch // 2, pair_body, 0)

    @pl.when(nch % 2 == 1)
    def _():
        step(nch - 1, buf_a, None)

    _finish_heads(B_HEADS, acc_ref, o_ref)


def _fox_attention(q, ka, vt, qgain):
    b, tp, _ = q.shape
    tk = ka.shape[1]
    qblk = lambda bi, j: (bi, j, 0)
    return pl.pallas_call(
        _fox_kernel,
        grid=(b, tp // BLOCK),
        in_specs=[
            pl.BlockSpec((1, BLOCK, 1024), qblk),
            pl.BlockSpec((1, tk, B_HEADS * LANES), lambda bi, j: (bi, 0, 0)),
            pl.BlockSpec((1,) + vt.shape[1:], lambda bi, j: (bi, 0, 0, 0)),
            pl.BlockSpec((HEAD_DIM, 1), lambda bi, j: (0, 0)),
        ],
        out_specs=pl.BlockSpec((1, BLOCK, 1024), qblk),
        out_shape=jax.ShapeDtypeStruct((b, tp, 1024), BF16),
        scratch_shapes=[
            pltpu.VMEM((B_HEADS // 2, 2 * LANES, 2 * BLOCK), BF16),
            pltpu.VMEM((2, CHUNK, B_HEADS * BLOCK), F32),
            pltpu.VMEM((CHUNK, B_HEADS * BLOCK), BF16),
            pltpu.VMEM((2, 1, B_HEADS * BLOCK), F32),
            pltpu.VMEM((1, B_HEADS * BLOCK), F32),
            pltpu.VMEM((1, B_HEADS * BLOCK), F32),
            pltpu.VMEM((B_HEADS, VROWS, BLOCK), F32),
        ],
        compiler_params=_cparams("parallel", "arbitrary"),
        name="fox_attn",
    )(q, ka, vt, qgain)


def _indicator(nheads, width):
    e = (jnp.arange(nheads * HEAD_DIM)[:, None] // HEAD_DIM == jnp.arange(width)[None, :])
    return e.astype(BF16), e.T.astype(BF16)


def _pad_cols(w, n):
    return jnp.pad(w, ((0, 0), (0, n - w.shape[1])))


def _placements():
    src = jnp.arange(B_HEADS * HEAD_DIM)
    dst = jnp.arange(B_HEADS * LANES)
    pk = (dst[None, :] == (src[:, None] // HEAD_DIM) * LANES + src[:, None] % HEAD_DIM).astype(BF16)
    head = jnp.arange(LANES)
    pc = jnp.stack([(dst[None, :] == head[:, None] * LANES + HEAD_DIM + i) & (head[:, None] < B_HEADS)
                    for i in range(3)]).astype(BF16)
    return pk, pc


def kernel(x, meta_tokens, attn_norm, mlp_norm, mlp_w1, mlp_w2, a_w_in, a_q_gain, a_k_gain, a_w_out,
           kv_norm, kv_w, kv_f_bias, kv_k_gain, b_w_q, b_q_gain, b_w_out):
    b, s, d = x.shape
    t = s + N_META
    tp = -(-t // BLOCK) * BLOCK
    tk = -(-tp // CHUNK) * CHUNK
    tm = tp // 4
    depth = attn_norm.shape[0]
    n_a = a_w_in.shape[0]
    assert d == D_MODEL and tm % 16 == 0 and min(TOPK, s // 4) == TOPK and tk <= 2 ** IDX_BITS

    meta = jnp.broadcast_to(meta_tokens.astype(x.dtype)[None], (b, N_META, d))
    h = jnp.pad(jnp.concatenate([meta, x], axis=1), ((0, 0), (0, tp - t), (0, 0)))

    inv = 1.0 / (ROPE_THETA ** (jnp.arange(0, HEAD_DIM, 2, dtype=F32) / HEAD_DIM))
    ang = jnp.arange(tp, dtype=F32)[:, None] * inv[None, :]
    cos, sin = jnp.cos(ang), jnp.sin(ang)
    c2 = jnp.concatenate([cos, cos, cos, cos], axis=1)
    s2 = jnp.concatenate([-sin, sin, -sin, sin], axis=1)
    cosT, sinT = cos.T, sin.T

    e4, et4 = _indicator(A_KV_HEADS, LANES)
    e16, et16 = _indicator(B_HEADS, LANES)

    shared = None
    for i in range(depth):
        g_attn = attn_norm[i][None, :]
        if i < n_a:
            w_all = _pad_cols(a_w_in[i], 2176).astype(BF16)
            kgain = jnp.tile(a_k_gain[i], A_KV_HEADS)[None, :]
            q, k, v, qi, ki, wi = _proj_a(h, g_attn, w_all, e4, et4, kgain, c2, s2, tm)
            kp, vt, kip = _dsa_layout(k, v, ki, tk)
            o = _dsa_attention(q, kp, vt, qi, kip, wi, cosT, sinT, a_q_gain[i][:, None])
            wo = a_w_out[i]
        else:
            if shared is None:
                w_all = _pad_cols(kv_w, 2176).astype(BF16)
                kgain = jnp.tile(kv_k_gain, B_HEADS)[None, :]
                fbias = jnp.pad(kv_f_bias, (0, LANES - B_HEADS))[None, :]
                k, v, lf = _shared_kv(h, kv_norm[None, :], w_all, e16, et16, kgain, fbias, tm)
                pk, pc = _placements()
                shared = _fox_layout(k, v, _cumsum_t(lf), pk, pc, tk)
            j = i - n_a
            q = _proj_q(h.reshape(b * tp, d), g_attn, b_w_q[j].astype(BF16), tm).reshape(b, tp, d)
            o = _fox_attention(q, shared[0], shared[1], b_q_gain[j][:, None])
            wo = b_w_out[j]
        h = _out_mlp(h.reshape(b * tp, d), o.reshape(b * tp, d), wo.astype(BF16), mlp_norm[i][None, :],
                     mlp_w1[i].astype(BF16), mlp_w2[i].astype(BF16), tm).reshape(b, tp, d)
    return h[:, N_META:N_META + s]
```

```python
import functools
import math

import jax
import jax.numpy as jnp
from jax import lax
from jax.experimental import pallas as pl
from jax.experimental.pallas import tpu as pltpu

F32 = jnp.float32
BF16 = jnp.bfloat16

D_MODEL = 1024
HEAD_DIM = 64
HALF = HEAD_DIM // 2
N_META = 16
BLOCK = 128
CHUNK = 256
PASS_ROWS = 2 * CHUNK
FOLD = 64
A_Q_HEADS = 16
A_KV_HEADS = 4
A_REP = A_Q_HEADS // A_KV_HEADS
IDX_HEADS = 8
B_HEADS = 16
D_FF = 4 * D_MODEL
FF_CHUNK = 1024
TOPK = 256
EPS = 1e-6
ROPE_THETA = 10000.0
LOG2E = math.log2(math.e)
NEG = -1e30
N_BISECT = 20
IDX_BITS = 12
TIE_WALK_MAX = 10
VROWS = 80
LANES = 128
VMEM_LIMIT = 52 * 1024 * 1024


def _cparams(*sem):
    return pltpu.CompilerParams(dimension_semantics=sem, vmem_limit_bytes=VMEM_LIMIT)


def _dot(a, b):
    return jnp.dot(a, b, preferred_element_type=F32)


def _split2(x):
    hi = x.astype(BF16)
    lo = (x - hi.astype(F32)).astype(BF16)
    return hi, lo


def _split3(x):
    hi = x.astype(BF16)
    r = x - hi.astype(F32)
    mid = r.astype(BF16)
    lo = (r - mid.astype(F32)).astype(BF16)
    return hi, mid, lo


def _rms_rows(h, g):
    ms = jnp.mean(h * h, axis=-1, keepdims=True)
    return h * lax.rsqrt(ms + EPS) * g


def _head_norm_nat(x, e_ref, et_ref, gain):
    x2 = x * x
    hi, lo = _split2(x2)
    ss = _dot(hi, e_ref[...]) + _dot(lo, e_ref[...])
    inv = lax.rsqrt(ss * (1.0 / HEAD_DIM) + EPS)
    ihi, ilo = _split2(inv)
    invx = _dot(ihi, et_ref[...]) + _dot(ilo, et_ref[...])
    return x * invx * gain


def _rope_nat(x, c2, s2):
    w = x.shape[1]
    reps = w // LANES
    c = c2 if reps == 1 else jnp.concatenate([c2] * reps, axis=1)
    s = s2 if reps == 1 else jnp.concatenate([s2] * reps, axis=1)
    lane = lax.broadcasted_iota(jnp.int32, x.shape, 1)
    first = (lane & HALF) == 0
    partner = jnp.where(first, pltpu.roll(x, w - HALF, 1), pltpu.roll(x, HALF, 1))
    return x * c + partner * s


def _proj_a_kernel(h_ref, g_ref, w_ref, e_ref, et_ref, kg_ref, c2_ref, s2_ref,
                   q_o, k_o, v_o, qi_o, ki_o, wi_o):
    hn = _rms_rows(h_ref[0], g_ref[...]).astype(BF16)
    y = _dot(hn, w_ref[...])
    q_o[0] = y[:, 0:1024].astype(BF16)
    c2 = c2_ref[...]
    s2 = s2_ref[...]
    k = _head_norm_nat(y[:, 1024:1280], e_ref, et_ref, kg_ref[...])
    k_o[0] = _rope_nat(k, c2, s2).astype(BF16)
    v_o[0] = y[:, 1280:1536].astype(BF16)
    qi_o[0] = y[:, 1536:2048].astype(BF16)
    kiw = y[:, 2048:2176]
    ki_o[0] = _rope_nat(kiw, c2, s2)[:, 0:HEAD_DIM].astype(BF16)
    wi_o[0] = kiw * (IDX_HEADS ** -0.5 * HEAD_DIM ** -0.5)


def _proj_a(h, g, w_all, e, et, kgain, c2, s2, tm):
    b, tp, d = h.shape
    nt = tp // tm
    wn = w_all.shape[1]
    const = lambda *_: (0, 0)
    tile = lambda bi, j: (bi, j, 0)
    return pl.pallas_call(
        _proj_a_kernel,
        grid=(b, nt),
        in_specs=[
            pl.BlockSpec((1, tm, d), tile),
            pl.BlockSpec((1, d), const),
            pl.BlockSpec((d, wn), const),
            pl.BlockSpec(e.shape, const),
            pl.BlockSpec(et.shape, const),
            pl.BlockSpec((1, 256), const),
            pl.BlockSpec((tm, LANES), lambda bi, j: (j, 0)),
            pl.BlockSpec((tm, LANES), lambda bi, j: (j, 0)),
        ],
        out_specs=[
            pl.BlockSpec((1, tm, 1024), tile),
            pl.BlockSpec((1, tm, 256), tile),
            pl.BlockSpec((1, tm, 256), tile),
            pl.BlockSpec((1, tm, 512), tile),
            pl.BlockSpec((1, tm, HEAD_DIM), tile),
            pl.BlockSpec((1, tm, LANES), tile),
        ],
        out_shape=[
            jax.ShapeDtypeStruct((b, tp, 1024), BF16),
            jax.ShapeDtypeStruct((b, tp, 256), BF16),
            jax.ShapeDtypeStruct((b, tp, 256), BF16),
            jax.ShapeDtypeStruct((b, tp, 512), BF16),
            jax.ShapeDtypeStruct((b, tp, HEAD_DIM), BF16),
            jax.ShapeDtypeStruct((b, tp, LANES), F32),
        ],
        compiler_params=_cparams("parallel", "parallel"),
        name="proj_a",
    )(h, g, w_all, e, et, kgain, c2, s2)


def _head_T(xT, h, cosT, sinT, gain, scale):
    x = xT[HEAD_DIM * h:HEAD_DIM * (h + 1)]
    if gain is not None:
        ss = jnp.sum(x * x, axis=0, keepdims=True)
        x = x * lax.rsqrt(ss * (1.0 / HEAD_DIM) + EPS) * gain
    if cosT is not None:
        x1 = x[:HALF]
        x2 = x[HALF:]
        x = jnp.concatenate([x1 * cosT - x2 * sinT, x1 * sinT + x2 * cosT], axis=0)
    if scale is not None:
        x = x * scale
    return x


def _update_max(m_ref, cm_ref, al_ref):
    m_old = m_ref[...]
    m_new = jnp.maximum(m_old, cm_ref[...])
    al_ref[...] = jnp.exp2(m_old - m_new)
    m_ref[...] = m_new


def _init_softmax(m_ref, acc_ref):
    m_ref[...] = jnp.full(m_ref.shape, NEG, F32)
    acc_ref[...] = jnp.zeros(acc_ref.shape, F32)


def _finish_heads(nheads, acc_ref, o_ref):
    outs = [acc_ref[h, 0:HEAD_DIM, :] / acc_ref[h, HEAD_DIM:HEAD_DIM + 1, :] for h in range(nheads)]
    oT = jnp.concatenate(outs, axis=0)
    o_ref[0] = oT.T.astype(BF16)


def _values_T(v, nheads):
    vT = v.T
    row = lax.broadcasted_iota(jnp.int32, (VROWS - HEAD_DIM, v.shape[0]), 0)
    ones = jnp.where(row == 0, 1.0, 0.0)
    parts = []
    for h in range(nheads):
        parts += [vT[HEAD_DIM * h:HEAD_DIM * (h + 1)], ones]
    return jnp.concatenate(parts, axis=0)


def _valid_rows(c, tp, shape):
    return c * CHUNK + lax.broadcasted_iota(jnp.int32, shape, 0) < tp


def _dsa_kernel(q_ref, k_ref, vt_ref, qi_ref, ki_ref, wi_ref, cos_ref, sin_ref, qg_ref,
                o_ref, sc_ref, tie_ref, qx_ref, qix_ref, thr_ref, cut_ref, bias_ref, s_ref, p_ref, cm_ref, m_ref,
                al_ref, acc_ref):
    blk = pl.program_id(1)
    nch = (blk + 2) // 2
    cosT = cos_ref[...]
    sinT = sin_ref[...]

    qT = q_ref[0].astype(F32).T
    qscale = HEAD_DIM ** -0.5 * LOG2E
    @pl.when(blk == 0)
    def _():
        qx_ref[...] = jnp.zeros(qx_ref.shape, BF16)

    for g in range(A_KV_HEADS):
        heads = [_head_T(qT, A_REP * g + j, cosT, sinT, qg_ref[...], qscale).astype(BF16)
                 for j in range(A_REP)]
        qx_ref[g, HEAD_DIM * g:HEAD_DIM * (g + 1), :] = jnp.concatenate(heads, axis=1)

    qiT = qi_ref[0].astype(F32).T
    qix_ref[...] = jnp.concatenate(
        [_head_T(qiT, h, cosT, sinT, None, None).astype(BF16) for h in range(IDX_HEADS)], axis=1)
    wT = wi_ref[0].T

    qidx = blk * BLOCK + lax.broadcasted_iota(jnp.int32, (CHUNK, BLOCK), 1)
    krow = lax.broadcasted_iota(jnp.int32, (CHUNK, BLOCK), 0)

    def idx_body(i, carry):
        offs = [pl.multiple_of(jnp.minimum(2 * i + j, nch - 1) * CHUNK, CHUNK) for j in range(2)]
        lgs = [_dot(ki_ref[0, pl.ds(off, CHUNK), :], qix_ref[...]) for off in offs]
        mx, mn = carry
        for off, lg in zip(offs, lgs):
            acc = jnp.zeros((CHUNK, BLOCK), F32)
            for h in range(IDX_HEADS):
                w_h = wT[HEAD_DIM + h:HEAD_DIM + h + 1, :]
                acc = acc + w_h * jnp.maximum(lg[:, BLOCK * h:BLOCK * (h + 1)], 0.0)
            causal = krow + off <= qidx
            hi_part = jnp.where(causal, acc, -jnp.inf)
            lo_part = jnp.where(causal, acc, jnp.inf)
            sc_ref[pl.ds(off, CHUNK), :] = hi_part
            mx = jnp.maximum(mx, jnp.max(hi_part.reshape(CHUNK // FOLD, FOLD, BLOCK), axis=0))
            mn = jnp.minimum(mn, jnp.min(lo_part.reshape(CHUNK // FOLD, FOLD, BLOCK), axis=0))
        return mx, mn

    mx, mn = lax.fori_loop(0, (nch + 1) // 2, idx_body, (jnp.full((FOLD, BLOCK), -jnp.inf, F32),
                                                         jnp.full((FOLD, BLOCK), jnp.inf, F32)))

    @pl.when(nch % 2 == 1)
    def _():
        sc_ref[pl.ds(pl.multiple_of(nch * CHUNK, CHUNK), CHUNK), :] = jnp.full((CHUNK, BLOCK), -jnp.inf, F32)

    thr_ref[...] = jnp.full((1, BLOCK), -jnp.inf, F32)
    cut_ref[...] = jnp.full((1, BLOCK), -1, jnp.int32)

    npass = (nch + 1) // 2
    prow = lax.broadcasted_iota(jnp.int32, (PASS_ROWS, BLOCK), 0)

    def sweep(fn, inits, combine, finish, src=sc_ref):
        def body(c, accs):
            off = pl.multiple_of(c * PASS_ROWS, PASS_ROWS)
            vals = fn(src[pl.ds(off, PASS_ROWS), :], off)
            return tuple(combine(a, v.reshape(PASS_ROWS // FOLD, FOLD, BLOCK)) for a, v in zip(accs, vals))
        return tuple(finish(a) for a in lax.fori_loop(0, npass, body, inits))

    def counts(*preds, src=sc_ref):
        return sweep(lambda s, off: tuple(jnp.where(p(s, off), 1.0, 0.0) for p in preds),
                     (jnp.zeros((FOLD, BLOCK), F32),) * len(preds),
                     lambda a, v: a + jnp.sum(v, axis=0),
                     lambda a: jnp.sum(a, axis=0, keepdims=True), src)

    def min_above(t):
        return sweep(lambda s, off: (jnp.where(s > t, s, jnp.inf),),
                     (jnp.full((FOLD, BLOCK), jnp.inf, F32),),
                     lambda a, v: jnp.minimum(a, jnp.min(v, axis=0)),
                     lambda a: jnp.min(a, axis=0, keepdims=True))[0]

    kf = float(TOPK)

    @pl.when(blk >= 2)
    def _():
        vmax = jnp.max(mx, axis=0, keepdims=True)
        vmin = jnp.min(mn, axis=0, keepdims=True)
        below = vmin - (jnp.abs(vmin) * 1e-3 + 1e-30)

        def bis(_, lohi):
            lo, hi = lohi
            mid = 0.5 * (lo + hi)
            ok = counts(lambda s, off: s > mid)[0] >= kf
            return jnp.where(ok, mid, lo), jnp.where(ok, hi, mid)

        lo, hi = lax.fori_loop(0, N_BISECT, bis, (below, vmax))

        def probe(lo):
            cand = min_above(lo)
            return (cand,) + counts(lambda s, off: s > cand, lambda s, off: s >= cand)

        def walk_cond(st):
            return jnp.max(jnp.where(st[2] >= kf, 1.0, 0.0)) > 0.0

        def walk_body(st):
            lo, cand, gt, _ = st
            lo = jnp.where(gt >= kf, cand, lo)
            return (lo,) + probe(lo)

        _, thr, cnt_gt, cnt_ge = lax.while_loop(walk_cond, walk_body, (lo,) + probe(lo))
        thr_ref[...] = thr
        cut_ref[...] = jnp.full((1, BLOCK), 2 ** IDX_BITS, jnp.int32)

        excess = (cnt_ge - kf).astype(jnp.int32)
        most = jnp.max(excess)

        def mark_ties():
            def mark(c, carry):
                off = pl.multiple_of(c * PASS_ROWS, PASS_ROWS)
                tied = sc_ref[pl.ds(off, PASS_ROWS), :] == thr
                tie_ref[pl.ds(off, PASS_ROWS), :] = jnp.where(tied, prow + off, 2 ** IDX_BITS)
                return carry

            lax.fori_loop(0, npass, mark, 0)

        @pl.when((most > 0) & (most <= TIE_WALK_MAX))
        def _():
            mark_ties()

            def drop(_, st):
                cut, rem = st
                top = sweep(lambda e, off: (jnp.where(e <= cut, e, -1),),
                            (jnp.full((FOLD, BLOCK), -1, jnp.int32),),
                            lambda a, v: jnp.maximum(a, jnp.max(v, axis=0)),
                            lambda a: jnp.max(a, axis=0, keepdims=True), tie_ref)[0]
                return jnp.where(rem > 0, top - 1, cut), rem - 1

            every = jnp.full((1, BLOCK), 2 ** IDX_BITS - 1, jnp.int32)
            cut_ref[...] = lax.fori_loop(0, most, drop, (every, excess))[0]

        @pl.when(most > TIE_WALK_MAX)
        def _():
            need = kf - cnt_gt
            mark_ties()

            grow = jnp.where(npass > 1, 1, 0) + jnp.where(npass > 2, 1, 0) + jnp.where(npass > 4, 1, 0)
            half0 = lax.shift_left(jnp.int32(PASS_ROWS // 2), grow)

            def tie_bis(i, lo_i):
                mid = lo_i + lax.shift_right_logical(half0, i)
                got = counts(lambda e, off: e <= mid, src=tie_ref)[0]
                return jnp.where(got >= need, lo_i, mid)

            nsteps = (PASS_ROWS.bit_length() - 1) + grow
            lo_i = lax.fori_loop(0, nsteps, tie_bis, jnp.full((1, BLOCK), -1, jnp.int32))
            cut_ref[...] = lo_i + 1

    _init_softmax(m_ref, acc_ref)

    def chunk_off(c):
        return pl.multiple_of(jnp.minimum(c, nch - 1) * CHUNK, CHUNK)

    def set_bias(c):
        off = chunk_off(c)
        s_idx = sc_ref[pl.ds(off, CHUNK), :]
        thr = thr_ref[...]
        sel = (s_idx > thr) | ((s_idx == thr) & (krow + off <= cut_ref[...]))
        bias_ref[...] = jnp.where(sel, 0.0, NEG)

    def scores(c, g, s_dst, cm_dst):
        sT = _dot(k_ref[0, pl.ds(chunk_off(c), CHUNK), :], qx_ref[g])
        bias = bias_ref[...]
        s = sT + jnp.concatenate([bias] * A_REP, axis=1)
        cols = slice(A_REP * BLOCK * g, A_REP * BLOCK * (g + 1))
        s_dst[:, cols] = s
        cm_dst[:, cols] = jnp.max(s, axis=0, keepdims=True)

    def probs(g, s_src):
        cols = slice(A_REP * BLOCK * g, A_REP * BLOCK * (g + 1))
        p_ref[:, cols] = jnp.exp2(s_src[:, cols] - m_ref[:, cols]).astype(BF16)

    def values(c, g):
        w = A_REP * BLOCK
        pv = _dot(vt_ref[0, c, VROWS * g:VROWS * (g + 1), :], p_ref[:, w * g:w * (g + 1)])
        for j in range(A_REP):
            h = A_REP * g + j
            acc_ref[h] = al_ref[:, BLOCK * h:BLOCK * (h + 1)] * acc_ref[h] + pv[:, BLOCK * j:BLOCK * (j + 1)]

    def step(c, src, dst):
        _update_max(m_ref, src[1], al_ref)
        if dst is not None:
            set_bias(c + 1)
        for g in range(A_KV_HEADS):
            if dst is not None:
                scores(c + 1, g, *dst)
            probs(g, src[0])
        for g in range(A_KV_HEADS):
            values(c, g)

    buf_a = (s_ref.at[0], cm_ref.at[0])
    buf_b = (s_ref.at[1], cm_ref.at[1])
    set_bias(0)
    for g in range(A_KV_HEADS):
        scores(0, g, *buf_a)

    def pair_body(i, carry):
        step(2 * i, buf_a, buf_b)
        step(2 * i + 1, buf_b, buf_a)
        return carry

    lax.fori_loop(0, nch // 2, pair_body, 0)

    @pl.when(nch % 2 == 1)
    def _():
        step(nch - 1, buf_a, None)

    _finish_heads(A_Q_HEADS, acc_ref, o_ref)


def _dsa_layout_kernel(tp, k_ref, v_ref, ki_ref, k_o, vt_o, ki_o):
    c = pl.program_id(1)
    k_o[0] = jnp.where(_valid_rows(c, tp, k_ref.shape[1:]), k_ref[0], 0)
    ki_o[0] = jnp.where(_valid_rows(c, tp, ki_ref.shape[1:]), ki_ref[0], 0)
    v = jnp.where(_valid_rows(c, tp, v_ref.shape[1:]), v_ref[0].astype(F32), 0.0)
    vt_o[0, 0] = _values_T(v, A_KV_HEADS).astype(BF16)


def _dsa_layout(k, v, ki, tk):
    b, tp, _ = k.shape
    nch = tk // CHUNK
    chunk = lambda bi, c: (bi, c, 0)
    return pl.pallas_call(
        functools.partial(_dsa_layout_kernel, tp),
        grid=(b, nch),
        in_specs=[
            pl.BlockSpec((1, CHUNK, 256), chunk),
            pl.BlockSpec((1, CHUNK, 256), chunk),
            pl.BlockSpec((1, CHUNK, HEAD_DIM), chunk),
        ],
        out_specs=[
            pl.BlockSpec((1, CHUNK, 256), chunk),
            pl.BlockSpec((1, 1, A_KV_HEADS * VROWS, CHUNK), lambda bi, c: (bi, c, 0, 0)),
            pl.BlockSpec((1, CHUNK, HEAD_DIM), chunk),
        ],
        out_shape=[
            jax.ShapeDtypeStruct((b, tk, 256), BF16),
            jax.ShapeDtypeStruct((b, nch, A_KV_HEADS * VROWS, CHUNK), BF16),
            jax.ShapeDtypeStruct((b, tk, HEAD_DIM), BF16),
        ],
        compiler_params=_cparams("parallel", "parallel"),
        name="dsa_layout",
    )(k, v, ki)


def _dsa_attention(q, k, vt, qi, ki, wi, cosT, sinT, qgain):
    b, tp, _ = q.shape
    tk = k.shape[1]
    nblk = tp // BLOCK
    qblk = lambda bi, j: (bi, j, 0)
    seq = lambda bi, j: (bi, 0, 0)
    return pl.pallas_call(
        _dsa_kernel,
        grid=(b, nblk),
        in_specs=[
            pl.BlockSpec((1, BLOCK, 1024), qblk),
            pl.BlockSpec((1, tk, 256), seq),
            pl.BlockSpec((1,) + vt.shape[1:], lambda bi, j: (bi, 0, 0, 0)),
            pl.BlockSpec((1, BLOCK, 512), qblk),
            pl.BlockSpec((1, tk, HEAD_DIM), seq),
            pl.BlockSpec((1, BLOCK, LANES), qblk),
            pl.BlockSpec((HALF, BLOCK), lambda bi, j: (0, j)),
            pl.BlockSpec((HALF, BLOCK), lambda bi, j: (0, j)),
            pl.BlockSpec((HEAD_DIM, 1), lambda bi, j: (0, 0)),
        ],
        out_specs=pl.BlockSpec((1, BLOCK, 1024), qblk),
        out_shape=jax.ShapeDtypeStruct((b, tp, 1024), BF16),
        scratch_shapes=[
            pltpu.VMEM((-(-tk // PASS_ROWS) * PASS_ROWS, BLOCK), F32),
            pltpu.VMEM((-(-tk // PASS_ROWS) * PASS_ROWS, BLOCK), jnp.int32),
            pltpu.VMEM((A_KV_HEADS, 256, A_REP * BLOCK), BF16),
            pltpu.VMEM((HEAD_DIM, IDX_HEADS * BLOCK), BF16),
            pltpu.VMEM((1, BLOCK), F32),
            pltpu.VMEM((1, BLOCK), jnp.int32),
            pltpu.VMEM((CHUNK, BLOCK), F32),
            pltpu.VMEM((2, CHUNK, A_Q_HEADS * BLOCK), F32),
            pltpu.VMEM((CHUNK, A_Q_HEADS * BLOCK), BF16),
            pltpu.VMEM((2, 1, A_Q_HEADS * BLOCK), F32),
            pltpu.VMEM((1, A_Q_HEADS * BLOCK), F32),
            pltpu.VMEM((1, A_Q_HEADS * BLOCK), F32),
            pltpu.VMEM((A_Q_HEADS, VROWS, BLOCK), F32),
        ],
        compiler_params=_cparams("parallel", "arbitrary"),
        name="dsa_attn",
    )(q, k, vt, qi, ki, wi, cosT, sinT, qgain)


def _mlp_kernel(h_ref, o_ref, wo_ref, g_ref, w1_ref, w2_ref, out_ref):
    h1 = h_ref[...] + _dot(o_ref[...], wo_ref[...])
    hn = _rms_rows(h1, g_ref[...]).astype(BF16)
    acc = h1
    for c in range(D_FF // FF_CHUNK):
        u = jnp.maximum(_dot(hn, w1_ref[:, FF_CHUNK * c:FF_CHUNK * (c + 1)]), 0.0)
        acc = acc + _dot((u * u).astype(BF16), w2_ref[FF_CHUNK * c:FF_CHUNK * (c + 1), :])
    out_ref[...] = acc


def _out_mlp(h2, o2, wo, g, w1, w2, tm):
    n, d = h2.shape
    const = lambda i: (0, 0)
    tile = lambda i: (i, 0)
    once = pl.Buffered(1)
    return pl.pallas_call(
        _mlp_kernel,
        grid=(n // tm,),
        in_specs=[
            pl.BlockSpec((tm, d), tile),
            pl.BlockSpec((tm, d), tile),
            pl.BlockSpec((d, d), const, pipeline_mode=once),
            pl.BlockSpec((1, d), const),
            pl.BlockSpec((d, D_FF), const, pipeline_mode=once),
            pl.BlockSpec((D_FF, d), const, pipeline_mode=once),
        ],
        out_specs=pl.BlockSpec((tm, d), tile),
        out_shape=jax.ShapeDtypeStruct((n, d), F32),
        compiler_params=_cparams("parallel"),
        name="out_mlp",
    )(h2, o2, wo, g, w1, w2)


def _shared_kv_kernel(h_ref, g_ref, w_ref, e_ref, et_ref, kg_ref, fb_ref, k_o, v_o, lf_o):
    hn = _rms_rows(h_ref[0], g_ref[...]).astype(BF16)
    y = _dot(hn, w_ref[...])
    k_o[0] = _head_norm_nat(y[:, 0:1024], e_ref, et_ref, kg_ref[...]).astype(BF16)
    v_o[0] = y[:, 1024:2048].astype(BF16)
    x = y[:, 2048:2176] + fb_ref[...]
    lf_o[0] = jnp.minimum(x, 0.0) - jnp.log1p(jnp.exp(-jnp.abs(x)))


def _shared_kv(h, g, w_all, e, et, kgain, fbias, tm):
    b, tp, d = h.shape
    const = lambda *_: (0, 0)
    tile = lambda bi, j: (bi, j, 0)
    return pl.pallas_call(
        _shared_kv_kernel,
        grid=(b, tp // tm),
        in_specs=[
            pl.BlockSpec((1, tm, d), tile),
            pl.BlockSpec((1, d), const),
            pl.BlockSpec(w_all.shape, const),
            pl.BlockSpec(e.shape, const),
            pl.BlockSpec(et.shape, const),
            pl.BlockSpec((1, 1024), const),
            pl.BlockSpec((1, LANES), const),
        ],
        out_specs=[
            pl.BlockSpec((1, tm, 1024), tile),
            pl.BlockSpec((1, tm, 1024), tile),
            pl.BlockSpec((1, tm, LANES), tile),
        ],
        out_shape=[
            jax.ShapeDtypeStruct((b, tp, 1024), BF16),
            jax.ShapeDtypeStruct((b, tp, 1024), BF16),
            jax.ShapeDtypeStruct((b, tp, LANES), F32),
        ],
        compiler_params=_cparams("parallel", "parallel"),
        name="shared_kv",
    )(h, g, w_all, e, et, kgain, fbias)


def _proj_q_kernel(h_ref, g_ref, w_ref, q_o):
    hn = _rms_rows(h_ref[...], g_ref[...]).astype(BF16)
    q_o[...] = _dot(hn, w_ref[...]).astype(BF16)


def _proj_q(h2, g, w, tm):
    n, d = h2.shape
    const = lambda i: (0, 0)
    tile = lambda i: (i, 0)
    return pl.pallas_call(
        _proj_q_kernel,
        grid=(n // tm,),
        in_specs=[pl.BlockSpec((tm, d), tile), pl.BlockSpec((1, d), const), pl.BlockSpec((d, d), const)],
        out_specs=pl.BlockSpec((tm, d), tile),
        out_shape=jax.ShapeDtypeStruct((n, d), BF16),
        compiler_params=_cparams("parallel"),
        name="proj_q",
    )(h2, g, w)


def _fox_layout_kernel(tp, k_ref, v_ref, lf_ref, pc_ref, ka_o, vt_o, carry_ref):
    ci = pl.program_id(1)

    @pl.when(ci == 0)
    def _():
        carry_ref[...] = jnp.zeros(carry_ref.shape, F32)

    lf = jnp.where(_valid_rows(ci, tp, lf_ref.shape[1:]), lf_ref[0], 0.0)
    r = lax.broadcasted_iota(jnp.int32, (CHUNK, CHUNK), 0)
    cidx = lax.broadcasted_iota(jnp.int32, (CHUNK, CHUNK), 1)
    tri = jnp.where(cidx <= r, 1.0, 0.0).astype(BF16)
    hi, mid, lo = _split3(lf)
    c = _dot(tri, hi) + _dot(tri, mid) + _dot(tri, lo) + carry_ref[...]
    carry_ref[...] = c[CHUNK - 1:CHUNK, :]

    hi, mid, lo = _split3(c * LOG2E)
    lane = lax.broadcasted_iota(jnp.int32, (CHUNK, LANES), 1)
    packed = jnp.where(lane < B_HEADS, hi.astype(F32),
                       jnp.where(lane < 2 * B_HEADS, pltpu.roll(mid.astype(F32), B_HEADS, 1),
                                 pltpu.roll(lo.astype(F32), 2 * B_HEADS, 1)))
    pieces = _dot(packed.astype(BF16), pc_ref[...])
    k = jnp.where(_valid_rows(ci, tp, k_ref.shape[1:]), k_ref[0], 0)
    first = lax.broadcasted_iota(jnp.int32, (CHUNK, LANES), 1) < HEAD_DIM
    cols = []
    for i in range(B_HEADS // 2):
        kc = k[:, LANES * i:LANES * (i + 1)].astype(F32)
        cols += [jnp.where(first, kc, 0.0), jnp.where(first, pltpu.roll(kc, HEAD_DIM, 1), 0.0)]
    ka_o[0] = (jnp.concatenate(cols, axis=1) + pieces).astype(BF16)
    v = jnp.where(_valid_rows(ci, tp, v_ref.shape[1:]), v_ref[0].astype(F32), 0.0)
    vt_o[0, 0] = _values_T(v, B_HEADS).astype(BF16)


def _fox_layout(k, v, lf, pc, tk):
    b, tp, _ = k.shape
    nch = tk // CHUNK
    chunk = lambda bi, ci: (bi, ci, 0)
    return pl.pallas_call(
        functools.partial(_fox_layout_kernel, tp),
        grid=(b, nch),
        in_specs=[
            pl.BlockSpec((1, CHUNK, 1024), chunk),
            pl.BlockSpec((1, CHUNK, 1024), chunk),
            pl.BlockSpec((1, CHUNK, LANES), chunk),
            pl.BlockSpec(pc.shape, lambda bi, ci: (0, 0)),
        ],
        out_specs=[
            pl.BlockSpec((1, CHUNK, B_HEADS * LANES), chunk),
            pl.BlockSpec((1, 1, B_HEADS * VROWS, CHUNK), lambda bi, ci: (bi, ci, 0, 0)),
        ],
        out_shape=[
            jax.ShapeDtypeStruct((b, tk, B_HEADS * LANES), BF16),
            jax.ShapeDtypeStruct((b, nch, B_HEADS * VROWS, CHUNK), BF16),
        ],
        scratch_shapes=[pltpu.VMEM((1, LANES), F32)],
        compiler_params=_cparams("parallel", "arbitrary"),
        name="fox_layout",
    )(k, v, lf, pc)


def _fox_kernel(q_ref, ka_ref, vt_ref, qg_ref, o_ref, qx_ref, s_ref, p_ref, cm_ref, m_ref, al_ref, acc_ref):
    blk = pl.program_id(1)
    nch = (blk + 2) // 2
    qT = q_ref[0].astype(F32).T
    qscale = HEAD_DIM ** -0.5 * LOG2E
    row = lax.broadcasted_iota(jnp.int32, (HEAD_DIM, BLOCK), 0)
    @pl.when(blk == 0)
    def _():
        minus3 = jnp.where(row < 3, -1.0, 0.0).astype(BF16)
        zeros = jnp.zeros((LANES, BLOCK), BF16)
        left = jnp.concatenate([zeros[:HEAD_DIM], minus3, zeros], axis=0)
        right = jnp.concatenate([zeros, zeros[:HEAD_DIM], minus3], axis=0)
        for pr in range(B_HEADS // 2):
            qx_ref[pr] = jnp.concatenate([left, right], axis=1)

    for pr in range(B_HEADS // 2):
        for j in range(2):
            x = _head_T(qT, 2 * pr + j, None, None, qg_ref[...], qscale).astype(BF16)
            qx_ref[pr, 2 * HEAD_DIM * j:2 * HEAD_DIM * j + HEAD_DIM, BLOCK * j:BLOCK * (j + 1)] = x

    _init_softmax(m_ref, acc_ref)
    qidx = blk * BLOCK + lax.broadcasted_iota(jnp.int32, (CHUNK, BLOCK), 1)
    krow = lax.broadcasted_iota(jnp.int32, (CHUNK, BLOCK), 0)

    npair = B_HEADS // 2

    def scores(c, pr, s_dst, cm_dst):
        off = pl.multiple_of(jnp.minimum(c, nch - 1) * CHUNK, CHUNK)
        bias = jnp.where(krow + off <= qidx, 0.0, NEG)
        sT = _dot(ka_ref[0, pl.ds(off, CHUNK), 2 * LANES * pr:2 * LANES * (pr + 1)], qx_ref[pr])
        s = sT + jnp.concatenate([bias, bias], axis=1)
        cols = slice(2 * BLOCK * pr, 2 * BLOCK * (pr + 1))
        s_dst[:, cols] = s
        cm_dst[:, cols] = jnp.max(s, axis=0, keepdims=True)

    def probs(pr, s_src):
        cols = slice(2 * BLOCK * pr, 2 * BLOCK * (pr + 1))
        p_ref[:, cols] = jnp.exp2(s_src[:, cols] - m_ref[:, cols]).astype(BF16)

    def values(c, pr):
        pv = _dot(vt_ref[0, c, 2 * VROWS * pr:2 * VROWS * (pr + 1), :],
                  p_ref[:, 2 * BLOCK * pr:2 * BLOCK * (pr + 1)])
        for j in range(2):
            h = 2 * pr + j
            acc_ref[h] = (al_ref[:, BLOCK * h:BLOCK * (h + 1)] * acc_ref[h]
                          + pv[VROWS * j:VROWS * (j + 1), BLOCK * j:BLOCK * (j + 1)])

    def step(c, src, dst):
        _update_max(m_ref, src[1], al_ref)
        for pr in range(npair):
            if dst is not None:
                scores(c + 1, pr, *dst)
            probs(pr, src[0])
            if pr > 0:
                values(c, pr - 1)
        values(c, npair - 1)

    buf_a = (s_ref.at[0], cm_ref.at[0])
    buf_b = (s_ref.at[1], cm_ref.at[1])
    for pr in range(npair):
        scores(0, pr, *buf_a)

    def pair_body(i, carry):
        step(2 * i, buf_a, buf_b)
        step(2 * i + 1, buf_b, buf_a)
        return carry

    lax.fori_loop(0, nch // 2, pair_body, 0)

    @pl.when(nch % 2 == 1)
    def _():
        step(nch - 1, buf_a, None)

    _finish_heads(B_HEADS, acc_ref, o_ref)


def _fox_attention(q, ka, vt, qgain):
    b, tp, _ = q.shape
    tk = ka.shape[1]
    qblk = lambda bi, j: (bi, j, 0)
    return pl.pallas_call(
        _fox_kernel,
        grid=(b, tp // BLOCK),
        in_specs=[
            pl.BlockSpec((1, BLOCK, 1024), qblk),
            pl.BlockSpec((1, tk, B_HEADS * LANES), lambda bi, j: (bi, 0, 0)),
            pl.BlockSpec((1,) + vt.shape[1:], lambda bi, j: (bi, 0, 0, 0)),
            pl.BlockSpec((HEAD_DIM, 1), lambda bi, j: (0, 0)),
        ],
        out_specs=pl.BlockSpec((1, BLOCK, 1024), qblk),
        out_shape=jax.ShapeDtypeStruct((b, tp, 1024), BF16),
        scratch_shapes=[
            pltpu.VMEM((B_HEADS // 2, 2 * LANES, 2 * BLOCK), BF16),
            pltpu.VMEM((2, CHUNK, B_HEADS * BLOCK), F32),
            pltpu.VMEM((CHUNK, B_HEADS * BLOCK), BF16),
            pltpu.VMEM((2, 1, B_HEADS * BLOCK), F32),
            pltpu.VMEM((1, B_HEADS * BLOCK), F32),
            pltpu.VMEM((1, B_HEADS * BLOCK), F32),
            pltpu.VMEM((B_HEADS, VROWS, BLOCK), F32),
        ],
        compiler_params=_cparams("parallel", "arbitrary"),
        name="fox_attn",
    )(q, ka, vt, qgain)


def _indicator(nheads, width):
    e = (jnp.arange(nheads * HEAD_DIM)[:, None] // HEAD_DIM == jnp.arange(width)[None, :])
    return e.astype(BF16), e.T.astype(BF16)


def _pad_cols(w, n):
    return jnp.pad(w, ((0, 0), (0, n - w.shape[1])))


def _placements():
    dst = jnp.arange(B_HEADS * LANES)[None, :]
    src = jnp.arange(LANES)[:, None]
    piece, head = src // B_HEADS, src % B_HEADS
    return ((dst == head * LANES + HEAD_DIM + piece) & (piece < 3)).astype(BF16)


def kernel(x, meta_tokens, attn_norm, mlp_norm, mlp_w1, mlp_w2, a_w_in, a_q_gain, a_k_gain, a_w_out,
           kv_norm, kv_w, kv_f_bias, kv_k_gain, b_w_q, b_q_gain, b_w_out):
    b, s, d = x.shape
    t = s + N_META
    tp = -(-t // BLOCK) * BLOCK
    tk = -(-tp // CHUNK) * CHUNK
    tm = tp // 4
    depth = attn_norm.shape[0]
    n_a = a_w_in.shape[0]
    assert d == D_MODEL and tm % 16 == 0 and min(TOPK, s // 4) == TOPK and tk <= 2 ** IDX_BITS

    meta = jnp.broadcast_to(meta_tokens.astype(x.dtype)[None], (b, N_META, d))
    h = jnp.pad(jnp.concatenate([meta, x], axis=1), ((0, 0), (0, tp - t), (0, 0)))

    inv = 1.0 / (ROPE_THETA ** (jnp.arange(0, HEAD_DIM, 2, dtype=F32) / HEAD_DIM))
    ang = jnp.arange(tp, dtype=F32)[:, None] * inv[None, :]
    cos, sin = jnp.cos(ang), jnp.sin(ang)
    c2 = jnp.concatenate([cos, cos, cos, cos], axis=1)
    s2 = jnp.concatenate([-sin, sin, -sin, sin], axis=1)
    cosT, sinT = cos.T, sin.T

    e4, et4 = _indicator(A_KV_HEADS, LANES)
    e16, et16 = _indicator(B_HEADS, LANES)

    shared = None
    for i in range(depth):
        g_attn = attn_norm[i][None, :]
        if i < n_a:
            w_all = _pad_cols(a_w_in[i], 2176).astype(BF16)
            kgain = jnp.tile(a_k_gain[i], A_KV_HEADS)[None, :]
            q, k, v, qi, ki, wi = _proj_a(h, g_attn, w_all, e4, et4, kgain, c2, s2, tm)
            kp, vt, kip = _dsa_layout(k, v, ki, tk)
            o = _dsa_attention(q, kp, vt, qi, kip, wi, cosT, sinT, a_q_gain[i][:, None])
            wo = a_w_out[i]
        else:
            if shared is None:
                w_all = _pad_cols(kv_w, 2176).astype(BF16)
                kgain = jnp.tile(kv_k_gain, B_HEADS)[None, :]
                fbias = jnp.pad(kv_f_bias, (0, LANES - B_HEADS))[None, :]
                k, v, lf = _shared_kv(h, kv_norm[None, :], w_all, e16, et16, kgain, fbias, tm)
                shared = _fox_layout(k, v, lf, _placements(), tk)
            j = i - n_a
            q = _proj_q(h.reshape(b * tp, d), g_attn, b_w_q[j].astype(BF16), tm).reshape(b, tp, d)
            o = _fox_attention(q, shared[0], shared[1], b_q_gain[j][:, None])
            wo = b_w_out[j]
        h = _out_mlp(h.reshape(b * tp, d), o.reshape(b * tp, d), wo.astype(BF16), mlp_norm[i][None, :],
                     mlp_w1[i].astype(BF16), mlp_w2[i].astype(BF16), tm).reshape(b, tp, d)
    return h[:, N_META:N_META + s]
```

```python
import functools
import math

import jax
import jax.numpy as jnp
from jax import lax
from jax.experimental import pallas as pl
from jax.experimental.pallas import tpu as pltpu

F32 = jnp.float32
BF16 = jnp.bfloat16

D_MODEL = 1024
HEAD_DIM = 64
HALF = HEAD_DIM // 2
N_META = 16
BLOCK = 128
CHUNK = 256
PASS_ROWS = 2 * CHUNK
FOLD = 64
A_Q_HEADS = 16
A_KV_HEADS = 4
A_REP = A_Q_HEADS // A_KV_HEADS
IDX_HEADS = 8
B_HEADS = 16
D_FF = 4 * D_MODEL
FF_CHUNK = 1024
TOPK = 256
EPS = 1e-6
ROPE_THETA = 10000.0
LOG2E = math.log2(math.e)
NEG = -1e30
N_BISECT = 17
IDX_BITS = 12
TIE_WALK_MAX = 10
VROWS = 80
LANES = 128
VMEM_LIMIT = 52 * 1024 * 1024


def _cparams(*sem):
    return pltpu.CompilerParams(dimension_semantics=sem, vmem_limit_bytes=VMEM_LIMIT)


def _dot(a, b):
    return jnp.dot(a, b, preferred_element_type=F32)


def _split2(x):
    hi = x.astype(BF16)
    lo = (x - hi.astype(F32)).astype(BF16)
    return hi, lo


def _split3(x):
    hi = x.astype(BF16)
    r = x - hi.astype(F32)
    mid = r.astype(BF16)
    lo = (r - mid.astype(F32)).astype(BF16)
    return hi, mid, lo


def _rms_rows(h, g):
    ms = jnp.mean(h * h, axis=-1, keepdims=True)
    return h * lax.rsqrt(ms + EPS) * g


def _head_norm_nat(x, e_ref, et_ref, gain):
    x2 = x * x
    hi, lo = _split2(x2)
    ss = _dot(hi, e_ref[...]) + _dot(lo, e_ref[...])
    inv = lax.rsqrt(ss * (1.0 / HEAD_DIM) + EPS)
    ihi, ilo = _split2(inv)
    invx = _dot(ihi, et_ref[...]) + _dot(ilo, et_ref[...])
    return x * invx * gain


def _rope_nat(x, c2, s2):
    w = x.shape[1]
    reps = w // LANES
    c = c2 if reps == 1 else jnp.concatenate([c2] * reps, axis=1)
    s = s2 if reps == 1 else jnp.concatenate([s2] * reps, axis=1)
    lane = lax.broadcasted_iota(jnp.int32, x.shape, 1)
    first = (lane & HALF) == 0
    partner = jnp.where(first, pltpu.roll(x, w - HALF, 1), pltpu.roll(x, HALF, 1))
    return x * c + partner * s


def _proj_a_kernel(h_ref, g_ref, w_ref, e_ref, et_ref, kg_ref, c2_ref, s2_ref,
                   q_o, k_o, v_o, qi_o, ki_o, wi_o):
    hn = _rms_rows(h_ref[0], g_ref[...]).astype(BF16)
    y = _dot(hn, w_ref[...])
    q_o[0] = y[:, 0:1024].astype(BF16)
    c2 = c2_ref[...]
    s2 = s2_ref[...]
    k = _head_norm_nat(y[:, 1024:1280], e_ref, et_ref, kg_ref[...])
    k_o[0] = _rope_nat(k, c2, s2).astype(BF16)
    v_o[0] = y[:, 1280:1536].astype(BF16)
    qi_o[0] = y[:, 1536:2048].astype(BF16)
    kiw = y[:, 2048:2176]
    ki_o[0] = _rope_nat(kiw, c2, s2)[:, 0:HEAD_DIM].astype(BF16)
    wi_o[0] = kiw * (IDX_HEADS ** -0.5 * HEAD_DIM ** -0.5)


def _proj_a(h, g, w_all, e, et, kgain, c2, s2, tm):
    b, tp, d = h.shape
    nt = tp // tm
    wn = w_all.shape[1]
    const = lambda *_: (0, 0)
    tile = lambda bi, j: (bi, j, 0)
    return pl.pallas_call(
        _proj_a_kernel,
        grid=(b, nt),
        in_specs=[
            pl.BlockSpec((1, tm, d), tile),
            pl.BlockSpec((1, d), const),
            pl.BlockSpec((d, wn), const),
            pl.BlockSpec(e.shape, const),
            pl.BlockSpec(et.shape, const),
            pl.BlockSpec((1, 256), const),
            pl.BlockSpec((tm, LANES), lambda bi, j: (j, 0)),
            pl.BlockSpec((tm, LANES), lambda bi, j: (j, 0)),
        ],
        out_specs=[
            pl.BlockSpec((1, tm, 1024), tile),
            pl.BlockSpec((1, tm, 256), tile),
            pl.BlockSpec((1, tm, 256), tile),
            pl.BlockSpec((1, tm, 512), tile),
            pl.BlockSpec((1, tm, HEAD_DIM), tile),
            pl.BlockSpec((1, tm, LANES), tile),
        ],
        out_shape=[
            jax.ShapeDtypeStruct((b, tp, 1024), BF16),
            jax.ShapeDtypeStruct((b, tp, 256), BF16),
            jax.ShapeDtypeStruct((b, tp, 256), BF16),
            jax.ShapeDtypeStruct((b, tp, 512), BF16),
            jax.ShapeDtypeStruct((b, tp, HEAD_DIM), BF16),
            jax.ShapeDtypeStruct((b, tp, LANES), F32),
        ],
        compiler_params=_cparams("parallel", "parallel"),
        name="proj_a",
    )(h, g, w_all, e, et, kgain, c2, s2)


def _head_T(xT, h, cosT, sinT, gain):
    x = xT[HEAD_DIM * h:HEAD_DIM * (h + 1)]
    if gain is not None:
        ss = jnp.sum(x * x, axis=0, keepdims=True)
        x = x * lax.rsqrt(ss * (1.0 / HEAD_DIM) + EPS) * gain
    if cosT is not None:
        x1 = x[:HALF]
        x2 = x[HALF:]
        x = jnp.concatenate([x1 * cosT - x2 * sinT, x1 * sinT + x2 * cosT], axis=0)
    return x


def _query_gain(qg_ref):
    return jnp.broadcast_to(qg_ref[...] * (HEAD_DIM ** -0.5 * LOG2E), (HEAD_DIM, BLOCK))


def _update_max(m_ref, cm_ref, al_ref):
    m_old = m_ref[...]
    m_new = jnp.maximum(m_old, cm_ref[...])
    al_ref[...] = jnp.exp2(m_old - m_new)
    m_ref[...] = m_new


def _init_softmax(m_ref, acc_ref):
    m_ref[...] = jnp.full(m_ref.shape, NEG, F32)
    acc_ref[...] = jnp.zeros(acc_ref.shape, F32)


def _finish_heads(nheads, acc_ref, o_ref):
    outs = [acc_ref[h, 0:HEAD_DIM, :] / acc_ref[h, HEAD_DIM:HEAD_DIM + 1, :] for h in range(nheads)]
    oT = jnp.concatenate(outs, axis=0)
    o_ref[0] = oT.T.astype(BF16)


def _values_T(v, nheads):
    vT = v.T
    row = lax.broadcasted_iota(jnp.int32, (VROWS - HEAD_DIM, v.shape[0]), 0)
    ones = jnp.where(row == 0, 1.0, 0.0)
    parts = []
    for h in range(nheads):
        parts += [vT[HEAD_DIM * h:HEAD_DIM * (h + 1)], ones]
    return jnp.concatenate(parts, axis=0)


def _valid_rows(c, tp, shape):
    return c * CHUNK + lax.broadcasted_iota(jnp.int32, shape, 0) < tp


def _dsa_kernel(q_ref, k_ref, vt_ref, qi_ref, ki_ref, wi_ref, cos_ref, sin_ref, qg_ref,
                o_ref, sc_ref, tie_ref, qx_ref, qix_ref, thr_ref, cut_ref, bias_ref, s_ref, p_ref, cm_ref, m_ref,
                al_ref, acc_ref):
    blk = pl.program_id(1)
    nch = (blk + 2) // 2
    cosT = cos_ref[...]
    sinT = sin_ref[...]

    @pl.when(blk == 0)
    def _():
        qx_ref[...] = jnp.zeros(qx_ref.shape, BF16)

    qiT = qi_ref[0].astype(F32).T
    qix_ref[...] = jnp.concatenate(
        [_head_T(qiT, h, cosT, sinT, None).astype(BF16) for h in range(IDX_HEADS)], axis=1)
    wT = wi_ref[0].T

    qidx = blk * BLOCK + lax.broadcasted_iota(jnp.int32, (CHUNK, BLOCK), 1)
    krow = lax.broadcasted_iota(jnp.int32, (CHUNK, BLOCK), 0)

    def idx_body(i, carry):
        offs = [pl.multiple_of(jnp.minimum(2 * i + j, nch - 1) * CHUNK, CHUNK) for j in range(2)]
        lgs = [_dot(ki_ref[0, pl.ds(off, CHUNK), :], qix_ref[...]) for off in offs]
        mx, mn = carry
        for off, lg in zip(offs, lgs):
            acc = jnp.zeros((CHUNK, BLOCK), F32)
            for h in range(IDX_HEADS):
                w_h = wT[HEAD_DIM + h:HEAD_DIM + h + 1, :]
                acc = acc + w_h * jnp.maximum(lg[:, BLOCK * h:BLOCK * (h + 1)], 0.0)
            causal = krow + off <= qidx
            hi_part = jnp.where(causal, acc, -jnp.inf)
            lo_part = jnp.where(causal, acc, jnp.inf)
            sc_ref[pl.ds(off, CHUNK), :] = hi_part
            mx = jnp.maximum(mx, jnp.max(hi_part.reshape(CHUNK // FOLD, FOLD, BLOCK), axis=0))
            mn = jnp.minimum(mn, jnp.min(lo_part.reshape(CHUNK // FOLD, FOLD, BLOCK), axis=0))
        return mx, mn

    mx, mn = lax.fori_loop(0, (nch + 1) // 2, idx_body, (jnp.full((FOLD, BLOCK), -jnp.inf, F32),
                                                         jnp.full((FOLD, BLOCK), jnp.inf, F32)))

    @pl.when(nch % 2 == 1)
    def _():
        sc_ref[pl.ds(pl.multiple_of(nch * CHUNK, CHUNK), CHUNK), :] = jnp.full((CHUNK, BLOCK), -jnp.inf, F32)

    thr_ref[...] = jnp.full((1, BLOCK), -jnp.inf, F32)
    cut_ref[...] = jnp.full((1, BLOCK), -1, jnp.int32)

    npass = (nch + 1) // 2
    prow = lax.broadcasted_iota(jnp.int32, (PASS_ROWS, BLOCK), 0)

    def sweep(fn, inits, combine, finish, src=sc_ref):
        def body(c, accs):
            off = pl.multiple_of(c * PASS_ROWS, PASS_ROWS)
            vals = fn(src[pl.ds(off, PASS_ROWS), :], off)
            return tuple(combine(a, v.reshape(PASS_ROWS // FOLD, FOLD, BLOCK)) for a, v in zip(accs, vals))
        return tuple(finish(a) for a in lax.fori_loop(0, npass, body, inits))

    def counts(*preds, src=sc_ref):
        return sweep(lambda s, off: tuple(jnp.where(p(s, off), 1.0, 0.0) for p in preds),
                     (jnp.zeros((FOLD, BLOCK), F32),) * len(preds),
                     lambda a, v: a + jnp.sum(v, axis=0),
                     lambda a: jnp.sum(a, axis=0, keepdims=True), src)

    def min_above(t):
        return sweep(lambda s, off: (jnp.where(s > t, s, jnp.inf),),
                     (jnp.full((FOLD, BLOCK), jnp.inf, F32),),
                     lambda a, v: jnp.minimum(a, jnp.min(v, axis=0)),
                     lambda a: jnp.min(a, axis=0, keepdims=True))[0]

    kf = float(TOPK)

    @pl.when(blk >= 2)
    def _():
        vmax = jnp.max(mx, axis=0, keepdims=True)
        vmin = jnp.min(mn, axis=0, keepdims=True)
        below = vmin - (jnp.abs(vmin) * 1e-3 + 1e-30)

        def bis(_, lohi):
            lo, hi = lohi
            mid = 0.5 * (lo + hi)
            ok = counts(lambda s, off: s > mid)[0] >= kf
            return jnp.where(ok, mid, lo), jnp.where(ok, hi, mid)

        lo, hi = lax.fori_loop(0, N_BISECT, bis, (below, vmax))

        def probe(lo):
            cand = min_above(lo)
            return (cand,) + counts(lambda s, off: s > cand, lambda s, off: s >= cand)

        def walk_cond(st):
            return jnp.max(jnp.where(st[2] >= kf, 1.0, 0.0)) > 0.0

        def walk_body(st):
            lo, cand, gt, _ = st
            lo = jnp.where(gt >= kf, cand, lo)
            return (lo,) + probe(lo)

        _, thr, cnt_gt, cnt_ge = lax.while_loop(walk_cond, walk_body, (lo,) + probe(lo))
        thr_ref[...] = thr
        cut_ref[...] = jnp.full((1, BLOCK), 2 ** IDX_BITS, jnp.int32)

        excess = (cnt_ge - kf).astype(jnp.int32)
        most = jnp.max(excess)

        def mark_ties():
            def mark(c, carry):
                off = pl.multiple_of(c * PASS_ROWS, PASS_ROWS)
                tied = sc_ref[pl.ds(off, PASS_ROWS), :] == thr
                tie_ref[pl.ds(off, PASS_ROWS), :] = jnp.where(tied, prow + off, 2 ** IDX_BITS)
                return carry

            lax.fori_loop(0, npass, mark, 0)

        @pl.when((most > 0) & (most <= TIE_WALK_MAX))
        def _():
            mark_ties()

            def drop(_, st):
                cut, rem = st
                top = sweep(lambda e, off: (jnp.where(e <= cut, e, -1),),
                            (jnp.full((FOLD, BLOCK), -1, jnp.int32),),
                            lambda a, v: jnp.maximum(a, jnp.max(v, axis=0)),
                            lambda a: jnp.max(a, axis=0, keepdims=True), tie_ref)[0]
                return jnp.where(rem > 0, top - 1, cut), rem - 1

            every = jnp.full((1, BLOCK), 2 ** IDX_BITS - 1, jnp.int32)
            cut_ref[...] = lax.fori_loop(0, most, drop, (every, excess))[0]

        @pl.when(most > TIE_WALK_MAX)
        def _():
            need = kf - cnt_gt
            mark_ties()

            grow = jnp.where(npass > 1, 1, 0) + jnp.where(npass > 2, 1, 0) + jnp.where(npass > 4, 1, 0)
            half0 = lax.shift_left(jnp.int32(PASS_ROWS // 2), grow)

            def tie_bis(i, lo_i):
                mid = lo_i + lax.shift_right_logical(half0, i)
                got = counts(lambda e, off: e <= mid, src=tie_ref)[0]
                return jnp.where(got >= need, lo_i, mid)

            nsteps = (PASS_ROWS.bit_length() - 1) + grow
            lo_i = lax.fori_loop(0, nsteps, tie_bis, jnp.full((1, BLOCK), -1, jnp.int32))
            cut_ref[...] = lo_i + 1

    _init_softmax(m_ref, acc_ref)

    def chunk_off(c):
        return pl.multiple_of(jnp.minimum(c, nch - 1) * CHUNK, CHUNK)

    def set_bias(c):
        off = chunk_off(c)
        s_idx = sc_ref[pl.ds(off, CHUNK), :]
        thr = thr_ref[...]
        sel = (s_idx > thr) | ((s_idx == thr) & (krow + off <= cut_ref[...]))
        bias_ref[...] = jnp.where(sel, 0.0, NEG)

    def scores(c, g, s_dst, cm_dst):
        sT = _dot(k_ref[0, pl.ds(chunk_off(c), CHUNK), :], qx_ref[g])
        bias = bias_ref[...]
        s = sT + jnp.concatenate([bias] * A_REP, axis=1)
        cols = slice(A_REP * BLOCK * g, A_REP * BLOCK * (g + 1))
        s_dst[:, cols] = s
        cm_dst[:, cols] = jnp.max(s, axis=0, keepdims=True)

    def probs(g, s_src):
        cols = slice(A_REP * BLOCK * g, A_REP * BLOCK * (g + 1))
        p_ref[:, cols] = jnp.exp2(s_src[:, cols] - m_ref[:, cols]).astype(BF16)

    def values(c, g):
        w = A_REP * BLOCK
        pv = _dot(vt_ref[0, c, VROWS * g:VROWS * (g + 1), :], p_ref[:, w * g:w * (g + 1)])
        for j in range(A_REP):
            h = A_REP * g + j
            acc_ref[h] = al_ref[:, BLOCK * h:BLOCK * (h + 1)] * acc_ref[h] + pv[:, BLOCK * j:BLOCK * (j + 1)]

    def step(c, src, dst):
        _update_max(m_ref, src[1], al_ref)
        if dst is not None:
            set_bias(c + 1)
        for g in range(A_KV_HEADS):
            if dst is not None:
                scores(c + 1, g, *dst)
            probs(g, src[0])
        for g in range(A_KV_HEADS):
            values(c, g)

    buf_a = (s_ref.at[0], cm_ref.at[0])
    buf_b = (s_ref.at[1], cm_ref.at[1])
    set_bias(0)
    qT = q_ref[0].astype(F32).T
    gain = _query_gain(qg_ref)
    for g in range(A_KV_HEADS):
        heads = [_head_T(qT, A_REP * g + j, cosT, sinT, gain).astype(BF16) for j in range(A_REP)]
        qx_ref[g, HEAD_DIM * g:HEAD_DIM * (g + 1), :] = jnp.concatenate(heads, axis=1)
        scores(0, g, *buf_a)

    def pair_body(i, carry):
        step(2 * i, buf_a, buf_b)
        step(2 * i + 1, buf_b, buf_a)
        return carry

    lax.fori_loop(0, nch // 2, pair_body, 0)

    @pl.when(nch % 2 == 1)
    def _():
        step(nch - 1, buf_a, None)

    _finish_heads(A_Q_HEADS, acc_ref, o_ref)


def _dsa_layout_kernel(tp, k_ref, v_ref, ki_ref, k_o, vt_o, ki_o):
    c = pl.program_id(1)
    k_o[0] = jnp.where(_valid_rows(c, tp, k_ref.shape[1:]), k_ref[0], 0)
    ki_o[0] = jnp.where(_valid_rows(c, tp, ki_ref.shape[1:]), ki_ref[0], 0)
    v = jnp.where(_valid_rows(c, tp, v_ref.shape[1:]), v_ref[0].astype(F32), 0.0)
    vt_o[0, 0] = _values_T(v, A_KV_HEADS).astype(BF16)


def _dsa_layout(k, v, ki, tk):
    b, tp, _ = k.shape
    nch = tk // CHUNK
    chunk = lambda bi, c: (bi, c, 0)
    return pl.pallas_call(
        functools.partial(_dsa_layout_kernel, tp),
        grid=(b, nch),
        in_specs=[
            pl.BlockSpec((1, CHUNK, 256), chunk),
            pl.BlockSpec((1, CHUNK, 256), chunk),
            pl.BlockSpec((1, CHUNK, HEAD_DIM), chunk),
        ],
        out_specs=[
            pl.BlockSpec((1, CHUNK, 256), chunk),
            pl.BlockSpec((1, 1, A_KV_HEADS * VROWS, CHUNK), lambda bi, c: (bi, c, 0, 0)),
            pl.BlockSpec((1, CHUNK, HEAD_DIM), chunk),
        ],
        out_shape=[
            jax.ShapeDtypeStruct((b, tk, 256), BF16),
            jax.ShapeDtypeStruct((b, nch, A_KV_HEADS * VROWS, CHUNK), BF16),
            jax.ShapeDtypeStruct((b, tk, HEAD_DIM), BF16),
        ],
        compiler_params=_cparams("parallel", "parallel"),
        name="dsa_layout",
    )(k, v, ki)


def _dsa_attention(q, k, vt, qi, ki, wi, cosT, sinT, qgain):
    b, tp, _ = q.shape
    tk = k.shape[1]
    nblk = tp // BLOCK
    qblk = lambda bi, j: (bi, j, 0)
    seq = lambda bi, j: (bi, 0, 0)
    return pl.pallas_call(
        _dsa_kernel,
        grid=(b, nblk),
        in_specs=[
            pl.BlockSpec((1, BLOCK, 1024), qblk),
            pl.BlockSpec((1, tk, 256), seq),
            pl.BlockSpec((1,) + vt.shape[1:], lambda bi, j: (bi, 0, 0, 0)),
            pl.BlockSpec((1, BLOCK, 512), qblk),
            pl.BlockSpec((1, tk, HEAD_DIM), seq),
            pl.BlockSpec((1, BLOCK, LANES), qblk),
            pl.BlockSpec((HALF, BLOCK), lambda bi, j: (0, j)),
            pl.BlockSpec((HALF, BLOCK), lambda bi, j: (0, j)),
            pl.BlockSpec((HEAD_DIM, 1), lambda bi, j: (0, 0)),
        ],
        out_specs=pl.BlockSpec((1, BLOCK, 1024), qblk),
        out_shape=jax.ShapeDtypeStruct((b, tp, 1024), BF16),
        scratch_shapes=[
            pltpu.VMEM((-(-tk // PASS_ROWS) * PASS_ROWS, BLOCK), F32),
            pltpu.VMEM((-(-tk // PASS_ROWS) * PASS_ROWS, BLOCK), jnp.int32),
            pltpu.VMEM((A_KV_HEADS, 256, A_REP * BLOCK), BF16),
            pltpu.VMEM((HEAD_DIM, IDX_HEADS * BLOCK), BF16),
            pltpu.VMEM((1, BLOCK), F32),
            pltpu.VMEM((1, BLOCK), jnp.int32),
            pltpu.VMEM((CHUNK, BLOCK), F32),
            pltpu.VMEM((2, CHUNK, A_Q_HEADS * BLOCK), F32),
            pltpu.VMEM((CHUNK, A_Q_HEADS * BLOCK), BF16),
            pltpu.VMEM((2, 1, A_Q_HEADS * BLOCK), F32),
            pltpu.VMEM((1, A_Q_HEADS * BLOCK), F32),
            pltpu.VMEM((1, A_Q_HEADS * BLOCK), F32),
            pltpu.VMEM((A_Q_HEADS, VROWS, BLOCK), F32),
        ],
        compiler_params=_cparams("parallel", "arbitrary"),
        name="dsa_attn",
    )(q, k, vt, qi, ki, wi, cosT, sinT, qgain)


def _mlp_kernel(h_ref, o_ref, wo_ref, g_ref, w1_ref, w2_ref, out_ref):
    h1 = h_ref[...] + _dot(o_ref[...], wo_ref[...])
    hn = _rms_rows(h1, g_ref[...]).astype(BF16)
    acc = h1
    for c in range(D_FF // FF_CHUNK):
        u = jnp.maximum(_dot(hn, w1_ref[:, FF_CHUNK * c:FF_CHUNK * (c + 1)]), 0.0)
        acc = acc + _dot((u * u).astype(BF16), w2_ref[FF_CHUNK * c:FF_CHUNK * (c + 1), :])
    out_ref[...] = acc


def _out_mlp(h2, o2, wo, g, w1, w2, tm):
    n, d = h2.shape
    const = lambda i: (0, 0)
    tile = lambda i: (i, 0)
    once = pl.Buffered(1)
    return pl.pallas_call(
        _mlp_kernel,
        grid=(n // tm,),
        in_specs=[
            pl.BlockSpec((tm, d), tile),
            pl.BlockSpec((tm, d), tile),
            pl.BlockSpec((d, d), const, pipeline_mode=once),
            pl.BlockSpec((1, d), const),
            pl.BlockSpec((d, D_FF), const, pipeline_mode=once),
            pl.BlockSpec((D_FF, d), const, pipeline_mode=once),
        ],
        out_specs=pl.BlockSpec((tm, d), tile),
        out_shape=jax.ShapeDtypeStruct((n, d), F32),
        compiler_params=_cparams("parallel"),
        name="out_mlp",
    )(h2, o2, wo, g, w1, w2)


def _shared_kv_kernel(h_ref, g_ref, w_ref, e_ref, et_ref, kg_ref, fb_ref, k_o, v_o, lf_o):
    hn = _rms_rows(h_ref[0], g_ref[...]).astype(BF16)
    y = _dot(hn, w_ref[...])
    k_o[0] = _head_norm_nat(y[:, 0:1024], e_ref, et_ref, kg_ref[...]).astype(BF16)
    v_o[0] = y[:, 1024:2048].astype(BF16)
    x = y[:, 2048:2176] + fb_ref[...]
    lf_o[0] = jnp.minimum(x, 0.0) - jnp.log1p(jnp.exp(-jnp.abs(x)))


def _shared_kv(h, g, w_all, e, et, kgain, fbias, tm):
    b, tp, d = h.shape
    const = lambda *_: (0, 0)
    tile = lambda bi, j: (bi, j, 0)
    return pl.pallas_call(
        _shared_kv_kernel,
        grid=(b, tp // tm),
        in_specs=[
            pl.BlockSpec((1, tm, d), tile),
            pl.BlockSpec((1, d), const),
            pl.BlockSpec(w_all.shape, const),
            pl.BlockSpec(e.shape, const),
            pl.BlockSpec(et.shape, const),
            pl.BlockSpec((1, 1024), const),
            pl.BlockSpec((1, LANES), const),
        ],
        out_specs=[
            pl.BlockSpec((1, tm, 1024), tile),
            pl.BlockSpec((1, tm, 1024), tile),
            pl.BlockSpec((1, tm, LANES), tile),
        ],
        out_shape=[
            jax.ShapeDtypeStruct((b, tp, 1024), BF16),
            jax.ShapeDtypeStruct((b, tp, 1024), BF16),
            jax.ShapeDtypeStruct((b, tp, LANES), F32),
        ],
        compiler_params=_cparams("parallel", "parallel"),
        name="shared_kv",
    )(h, g, w_all, e, et, kgain, fbias)


def _proj_q_kernel(h_ref, g_ref, w_ref, q_o):
    hn = _rms_rows(h_ref[...], g_ref[...]).astype(BF16)
    q_o[...] = _dot(hn, w_ref[...]).astype(BF16)


def _proj_q(h2, g, w, tm):
    n, d = h2.shape
    const = lambda i: (0, 0)
    tile = lambda i: (i, 0)
    return pl.pallas_call(
        _proj_q_kernel,
        grid=(n // tm,),
        in_specs=[pl.BlockSpec((tm, d), tile), pl.BlockSpec((1, d), const), pl.BlockSpec((d, d), const)],
        out_specs=pl.BlockSpec((tm, d), tile),
        out_shape=jax.ShapeDtypeStruct((n, d), BF16),
        compiler_params=_cparams("parallel"),
        name="proj_q",
    )(h2, g, w)


def _fox_layout_kernel(tp, k_ref, v_ref, lf_ref, pc_ref, ka_o, vt_o, carry_ref):
    ci = pl.program_id(1)

    @pl.when(ci == 0)
    def _():
        carry_ref[...] = jnp.zeros(carry_ref.shape, F32)

    lf = jnp.where(_valid_rows(ci, tp, lf_ref.shape[1:]), lf_ref[0], 0.0)
    r = lax.broadcasted_iota(jnp.int32, (CHUNK, CHUNK), 0)
    cidx = lax.broadcasted_iota(jnp.int32, (CHUNK, CHUNK), 1)
    tri = jnp.where(cidx <= r, 1.0, 0.0).astype(BF16)
    hi, mid, lo = _split3(lf)
    c = _dot(tri, hi) + _dot(tri, mid) + _dot(tri, lo) + carry_ref[...]
    carry_ref[...] = c[CHUNK - 1:CHUNK, :]

    hi, mid, lo = _split3(c * LOG2E)
    lane = lax.broadcasted_iota(jnp.int32, (CHUNK, LANES), 1)
    packed = jnp.where(lane < B_HEADS, hi.astype(F32),
                       jnp.where(lane < 2 * B_HEADS, pltpu.roll(mid.astype(F32), B_HEADS, 1),
                                 pltpu.roll(lo.astype(F32), 2 * B_HEADS, 1)))
    pieces = _dot(packed.astype(BF16), pc_ref[...])
    k = jnp.where(_valid_rows(ci, tp, k_ref.shape[1:]), k_ref[0], 0)
    first = lax.broadcasted_iota(jnp.int32, (CHUNK, LANES), 1) < HEAD_DIM
    cols = []
    for i in range(B_HEADS // 2):
        kc = k[:, LANES * i:LANES * (i + 1)].astype(F32)
        cols += [jnp.where(first, kc, 0.0), jnp.where(first, pltpu.roll(kc, HEAD_DIM, 1), 0.0)]
    ka_o[0] = (jnp.concatenate(cols, axis=1) + pieces).astype(BF16)
    v = jnp.where(_valid_rows(ci, tp, v_ref.shape[1:]), v_ref[0].astype(F32), 0.0)
    vt_o[0, 0] = _values_T(v, B_HEADS).astype(BF16)


def _fox_layout(k, v, lf, pc, tk):
    b, tp, _ = k.shape
    nch = tk // CHUNK
    chunk = lambda bi, ci: (bi, ci, 0)
    return pl.pallas_call(
        functools.partial(_fox_layout_kernel, tp),
        grid=(b, nch),
        in_specs=[
            pl.BlockSpec((1, CHUNK, 1024), chunk),
            pl.BlockSpec((1, CHUNK, 1024), chunk),
            pl.BlockSpec((1, CHUNK, LANES), chunk),
            pl.BlockSpec(pc.shape, lambda bi, ci: (0, 0)),
        ],
        out_specs=[
            pl.BlockSpec((1, CHUNK, B_HEADS * LANES), chunk),
            pl.BlockSpec((1, 1, B_HEADS * VROWS, CHUNK), lambda bi, ci: (bi, ci, 0, 0)),
        ],
        out_shape=[
            jax.ShapeDtypeStruct((b, tk, B_HEADS * LANES), BF16),
            jax.ShapeDtypeStruct((b, nch, B_HEADS * VROWS, CHUNK), BF16),
        ],
        scratch_shapes=[pltpu.VMEM((1, LANES), F32)],
        compiler_params=_cparams("parallel", "arbitrary"),
        name="fox_layout",
    )(k, v, lf, pc)


def _fox_kernel(q_ref, ka_ref, vt_ref, qg_ref, o_ref, qx_ref, s_ref, p_ref, cm_ref, m_ref, al_ref, acc_ref):
    blk = pl.program_id(1)
    nch = (blk + 2) // 2
    qT = q_ref[0].astype(F32).T
    gain = _query_gain(qg_ref)
    row = lax.broadcasted_iota(jnp.int32, (HEAD_DIM, BLOCK), 0)
    @pl.when(blk == 0)
    def _():
        minus3 = jnp.where(row < 3, -1.0, 0.0).astype(BF16)
        zeros = jnp.zeros((LANES, BLOCK), BF16)
        left = jnp.concatenate([zeros[:HEAD_DIM], minus3, zeros], axis=0)
        right = jnp.concatenate([zeros, zeros[:HEAD_DIM], minus3], axis=0)
        for pr in range(B_HEADS // 2):
            qx_ref[pr] = jnp.concatenate([left, right], axis=1)

    def prep_queries(pr):
        for j in range(2):
            x = _head_T(qT, 2 * pr + j, None, None, gain).astype(BF16)
            qx_ref[pr, 2 * HEAD_DIM * j:2 * HEAD_DIM * j + HEAD_DIM, BLOCK * j:BLOCK * (j + 1)] = x

    _init_softmax(m_ref, acc_ref)
    qidx = blk * BLOCK + lax.broadcasted_iota(jnp.int32, (CHUNK, BLOCK), 1)
    krow = lax.broadcasted_iota(jnp.int32, (CHUNK, BLOCK), 0)

    npair = B_HEADS // 2

    def scores(c, pr, s_dst, cm_dst):
        off = pl.multiple_of(jnp.minimum(c, nch - 1) * CHUNK, CHUNK)
        bias = jnp.where(krow + off <= qidx, 0.0, NEG)
        sT = _dot(ka_ref[0, pl.ds(off, CHUNK), 2 * LANES * pr:2 * LANES * (pr + 1)], qx_ref[pr])
        s = sT + jnp.concatenate([bias, bias], axis=1)
        cols = slice(2 * BLOCK * pr, 2 * BLOCK * (pr + 1))
        s_dst[:, cols] = s
        cm_dst[:, cols] = jnp.max(s, axis=0, keepdims=True)

    def probs(pr, s_src):
        cols = slice(2 * BLOCK * pr, 2 * BLOCK * (pr + 1))
        p_ref[:, cols] = jnp.exp2(s_src[:, cols] - m_ref[:, cols]).astype(BF16)

    def values(c, pr):
        pv = _dot(vt_ref[0, c, 2 * VROWS * pr:2 * VROWS * (pr + 1), :],
                  p_ref[:, 2 * BLOCK * pr:2 * BLOCK * (pr + 1)])
        for j in range(2):
            h = 2 * pr + j
            acc_ref[h] = (al_ref[:, BLOCK * h:BLOCK * (h + 1)] * acc_ref[h]
                          + pv[VROWS * j:VROWS * (j + 1), BLOCK * j:BLOCK * (j + 1)])

    def step(c, src, dst):
        _update_max(m_ref, src[1], al_ref)
        for pr in range(npair):
            if dst is not None:
                scores(c + 1, pr, *dst)
            probs(pr, src[0])
            if pr > 0:
                values(c, pr - 1)
        values(c, npair - 1)

    buf_a = (s_ref.at[0], cm_ref.at[0])
    buf_b = (s_ref.at[1], cm_ref.at[1])
    for pr in range(npair):
        prep_queries(pr)
        scores(0, pr, *buf_a)

    def pair_body(i, carry):
        step(2 * i, buf_a, buf_b)
        step(2 * i + 1, buf_b, buf_a)
        return carry

    lax.fori_loop(0, nch // 2, pair_body, 0)

    @pl.when(nch % 2 == 1)
    def _():
        step(nch - 1, buf_a, None)

    _finish_heads(B_HEADS, acc_ref, o_ref)


def _fox_attention(q, ka, vt, qgain):
    b, tp, _ = q.shape
    tk = ka.shape[1]
    qblk = lambda bi, j: (bi, j, 0)
    return pl.pallas_call(
        _fox_kernel,
        grid=(b, tp // BLOCK),
        in_specs=[
            pl.BlockSpec((1, BLOCK, 1024), qblk),
            pl.BlockSpec((1, tk, B_HEADS * LANES), lambda bi, j: (bi, 0, 0)),
            pl.BlockSpec((1,) + vt.shape[1:], lambda bi, j: (bi, 0, 0, 0)),
            pl.BlockSpec((HEAD_DIM, 1), lambda bi, j: (0, 0)),
        ],
        out_specs=pl.BlockSpec((1, BLOCK, 1024), qblk),
        out_shape=jax.ShapeDtypeStruct((b, tp, 1024), BF16),
        scratch_shapes=[
            pltpu.VMEM((B_HEADS // 2, 2 * LANES, 2 * BLOCK), BF16),
            pltpu.VMEM((2, CHUNK, B_HEADS * BLOCK), F32),
            pltpu.VMEM((CHUNK, B_HEADS * BLOCK), BF16),
            pltpu.VMEM((2, 1, B_HEADS * BLOCK), F32),
            pltpu.VMEM((1, B_HEADS * BLOCK), F32),
            pltpu.VMEM((1, B_HEADS * BLOCK), F32),
            pltpu.VMEM((B_HEADS, VROWS, BLOCK), F32),
        ],
        compiler_params=_cparams("parallel", "arbitrary"),
        name="fox_attn",
    )(q, ka, vt, qgain)


def _indicator(nheads, width):
    e = (jnp.arange(nheads * HEAD_DIM)[:, None] // HEAD_DIM == jnp.arange(width)[None, :])
    return e.astype(BF16), e.T.astype(BF16)


def _pad_cols(w, n):
    return jnp.pad(w, ((0, 0), (0, n - w.shape[1])))


def _placements():
    dst = jnp.arange(B_HEADS * LANES)[None, :]
    src = jnp.arange(LANES)[:, None]
    piece, head = src // B_HEADS, src % B_HEADS
    return ((dst == head * LANES + HEAD_DIM + piece) & (piece < 3)).astype(BF16)


def kernel(x, meta_tokens, attn_norm, mlp_norm, mlp_w1, mlp_w2, a_w_in, a_q_gain, a_k_gain, a_w_out,
           kv_norm, kv_w, kv_f_bias, kv_k_gain, b_w_q, b_q_gain, b_w_out):
    b, s, d = x.shape
    t = s + N_META
    tp = -(-t // BLOCK) * BLOCK
    tk = -(-tp // CHUNK) * CHUNK
    tm = tp // 4
    depth = attn_norm.shape[0]
    n_a = a_w_in.shape[0]
    assert d == D_MODEL and tm % 16 == 0 and min(TOPK, s // 4) == TOPK and tk <= 2 ** IDX_BITS

    meta = jnp.broadcast_to(meta_tokens.astype(x.dtype)[None], (b, N_META, d))
    h = jnp.pad(jnp.concatenate([meta, x], axis=1), ((0, 0), (0, tp - t), (0, 0)))

    inv = 1.0 / (ROPE_THETA ** (jnp.arange(0, HEAD_DIM, 2, dtype=F32) / HEAD_DIM))
    ang = jnp.arange(tp, dtype=F32)[:, None] * inv[None, :]
    cos, sin = jnp.cos(ang), jnp.sin(ang)
    c2 = jnp.concatenate([cos, cos, cos, cos], axis=1)
    s2 = jnp.concatenate([-sin, sin, -sin, sin], axis=1)
    cosT, sinT = cos.T, sin.T

    e4, et4 = _indicator(A_KV_HEADS, LANES)
    e16, et16 = _indicator(B_HEADS, LANES)

    shared = None
    for i in range(depth):
        g_attn = attn_norm[i][None, :]
        if i < n_a:
            w_all = _pad_cols(a_w_in[i], 2176).astype(BF16)
            kgain = jnp.tile(a_k_gain[i], A_KV_HEADS)[None, :]
            q, k, v, qi, ki, wi = _proj_a(h, g_attn, w_all, e4, et4, kgain, c2, s2, tm)
            kp, vt, kip = _dsa_layout(k, v, ki, tk)
            o = _dsa_attention(q, kp, vt, qi, kip, wi, cosT, sinT, a_q_gain[i][:, None])
            wo = a_w_out[i]
        else:
            if shared is None:
                w_all = _pad_cols(kv_w, 2176).astype(BF16)
                kgain = jnp.tile(kv_k_gain, B_HEADS)[None, :]
                fbias = jnp.pad(kv_f_bias, (0, LANES - B_HEADS))[None, :]
                k, v, lf = _shared_kv(h, kv_norm[None, :], w_all, e16, et16, kgain, fbias, tm)
                shared = _fox_layout(k, v, lf, _placements(), tk)
            j = i - n_a
            q = _proj_q(h.reshape(b * tp, d), g_attn, b_w_q[j].astype(BF16), tm).reshape(b, tp, d)
            o = _fox_attention(q, shared[0], shared[1], b_q_gain[j][:, None])
            wo = b_w_out[j]
        h = _out_mlp(h.reshape(b * tp, d), o.reshape(b * tp, d), wo.astype(BF16), mlp_norm[i][None, :],
                     mlp_w1[i].astype(BF16), mlp_w2[i].astype(BF16), tm).reshape(b, tp, d)
    return h[:, N_META:N_META + s]
```

```python
import functools
import math

import jax
import jax.numpy as jnp
from jax import lax
from jax.experimental import pallas as pl
from jax.experimental.pallas import tpu as pltpu

F32 = jnp.float32
BF16 = jnp.bfloat16

D_MODEL = 1024
HEAD_DIM = 64
HALF = HEAD_DIM // 2
N_META = 16
BLOCK = 128
CHUNK = 256
PASS_ROWS = 2 * CHUNK
FOLD = 64
A_Q_HEADS = 16
A_KV_HEADS = 4
A_REP = A_Q_HEADS // A_KV_HEADS
IDX_HEADS = 8
B_HEADS = 16
D_FF = 4 * D_MODEL
FF_CHUNK = 1024
TOPK = 256
EPS = 1e-6
ROPE_THETA = 10000.0
LOG2E = math.log2(math.e)
NEG = -1e30
N_BISECT = 17
IDX_BITS = 12
TIE_WALK_MAX = 10
VROWS = 80
LANES = 128
VMEM_LIMIT = 52 * 1024 * 1024


def _cparams(*sem):
    return pltpu.CompilerParams(dimension_semantics=sem, vmem_limit_bytes=VMEM_LIMIT)


def _dot(a, b):
    return jnp.dot(a, b, preferred_element_type=F32)


def _split2(x):
    hi = x.astype(BF16)
    lo = (x - hi.astype(F32)).astype(BF16)
    return hi, lo


def _split3(x):
    hi = x.astype(BF16)
    r = x - hi.astype(F32)
    mid = r.astype(BF16)
    lo = (r - mid.astype(F32)).astype(BF16)
    return hi, mid, lo


def _rms_rows(h, g):
    ms = jnp.mean(h * h, axis=-1, keepdims=True)
    return h * lax.rsqrt(ms + EPS) * g


def _head_norm_nat(x, e_ref, et_ref, gain):
    x2 = x * x
    hi, lo = _split2(x2)
    ss = _dot(hi, e_ref[...]) + _dot(lo, e_ref[...])
    inv = lax.rsqrt(ss * (1.0 / HEAD_DIM) + EPS)
    ihi, ilo = _split2(inv)
    invx = _dot(ihi, et_ref[...]) + _dot(ilo, et_ref[...])
    return x * invx * gain


def _rope_nat(x, c2, s2):
    w = x.shape[1]
    reps = w // LANES
    c = c2 if reps == 1 else jnp.concatenate([c2] * reps, axis=1)
    s = s2 if reps == 1 else jnp.concatenate([s2] * reps, axis=1)
    lane = lax.broadcasted_iota(jnp.int32, x.shape, 1)
    first = (lane & HALF) == 0
    partner = jnp.where(first, pltpu.roll(x, w - HALF, 1), pltpu.roll(x, HALF, 1))
    return x * c + partner * s


def _proj_a_kernel(h_ref, g_ref, w_ref, e_ref, et_ref, kg_ref, c2_ref, s2_ref,
                   q_o, k_o, v_o, qi_o, ki_o, wi_o):
    hn = _rms_rows(h_ref[0], g_ref[...]).astype(BF16)
    y = _dot(hn, w_ref[...])
    q_o[0] = y[:, 0:1024].astype(BF16)
    c2 = c2_ref[...]
    s2 = s2_ref[...]
    k = _head_norm_nat(y[:, 1024:1280], e_ref, et_ref, kg_ref[...])
    k_o[0] = _rope_nat(k, c2, s2).astype(BF16)
    v_o[0] = y[:, 1280:1536].astype(BF16)
    qi_o[0] = y[:, 1536:2048].astype(BF16)
    kiw = y[:, 2048:2176]
    ki_o[0] = _rope_nat(kiw, c2, s2)[:, 0:HEAD_DIM].astype(BF16)
    wi_o[0] = kiw * (IDX_HEADS ** -0.5 * HEAD_DIM ** -0.5)


def _proj_a(h, g, w_all, e, et, kgain, c2, s2, tm):
    b, tp, d = h.shape
    nt = tp // tm
    wn = w_all.shape[1]
    const = lambda *_: (0, 0)
    tile = lambda bi, j: (bi, j, 0)
    return pl.pallas_call(
        _proj_a_kernel,
        grid=(b, nt),
        in_specs=[
            pl.BlockSpec((1, tm, d), tile),
            pl.BlockSpec((1, d), const),
            pl.BlockSpec((d, wn), const),
            pl.BlockSpec(e.shape, const),
            pl.BlockSpec(et.shape, const),
            pl.BlockSpec((1, 256), const),
            pl.BlockSpec((tm, LANES), lambda bi, j: (j, 0)),
            pl.BlockSpec((tm, LANES), lambda bi, j: (j, 0)),
        ],
        out_specs=[
            pl.BlockSpec((1, tm, 1024), tile),
            pl.BlockSpec((1, tm, 256), tile),
            pl.BlockSpec((1, tm, 256), tile),
            pl.BlockSpec((1, tm, 512), tile),
            pl.BlockSpec((1, tm, HEAD_DIM), tile),
            pl.BlockSpec((1, tm, LANES), tile),
        ],
        out_shape=[
            jax.ShapeDtypeStruct((b, tp, 1024), BF16),
            jax.ShapeDtypeStruct((b, tp, 256), BF16),
            jax.ShapeDtypeStruct((b, tp, 256), BF16),
            jax.ShapeDtypeStruct((b, tp, 512), BF16),
            jax.ShapeDtypeStruct((b, tp, HEAD_DIM), BF16),
            jax.ShapeDtypeStruct((b, tp, LANES), F32),
        ],
        compiler_params=_cparams("parallel", "parallel"),
        name="proj_a",
    )(h, g, w_all, e, et, kgain, c2, s2)


def _head_T(xT, h, cosT, sinT, gain):
    x = xT[HEAD_DIM * h:HEAD_DIM * (h + 1)]
    if gain is not None:
        ss = jnp.sum(x * x, axis=0, keepdims=True)
        x = x * lax.rsqrt(ss * (1.0 / HEAD_DIM) + EPS) * gain
    if cosT is not None:
        x1 = x[:HALF]
        x2 = x[HALF:]
        x = jnp.concatenate([x1 * cosT - x2 * sinT, x1 * sinT + x2 * cosT], axis=0)
    return x


def _query_gain(qg_ref):
    return jnp.broadcast_to(qg_ref[...] * (HEAD_DIM ** -0.5 * LOG2E), (HEAD_DIM, BLOCK))


def _update_max(m_ref, cm_ref, al_ref):
    m_old = m_ref[...]
    m_new = jnp.maximum(m_old, cm_ref[...])
    al_ref[...] = jnp.exp2(m_old - m_new)
    m_ref[...] = m_new


def _init_softmax(m_ref, acc_ref):
    m_ref[...] = jnp.full(m_ref.shape, NEG, F32)
    acc_ref[...] = jnp.zeros(acc_ref.shape, F32)


def _finish_heads(nheads, acc_ref, o_ref):
    outs = [acc_ref[h, 0:HEAD_DIM, :] / acc_ref[h, HEAD_DIM:HEAD_DIM + 1, :] for h in range(nheads)]
    oT = jnp.concatenate(outs, axis=0)
    o_ref[0] = oT.T.astype(BF16)


def _values_T(v, nheads):
    vT = v.T
    row = lax.broadcasted_iota(jnp.int32, (VROWS - HEAD_DIM, v.shape[0]), 0)
    ones = jnp.where(row == 0, 1.0, 0.0)
    parts = []
    for h in range(nheads):
        parts += [vT[HEAD_DIM * h:HEAD_DIM * (h + 1)], ones]
    return jnp.concatenate(parts, axis=0)


def _valid_rows(c, tp, shape):
    return c * CHUNK + lax.broadcasted_iota(jnp.int32, shape, 0) < tp


def _dsa_kernel(q_ref, k_ref, vt_ref, qi_ref, ki_ref, wi_ref, cos_ref, sin_ref, qg_ref,
                o_ref, sc_ref, tie_ref, qx_ref, qix_ref, thr_ref, cut_ref, bias_ref, s_ref, p_ref, cm_ref, m_ref,
                al_ref, acc_ref):
    blk = pl.program_id(1)
    nch = (blk + 2) // 2
    cosT = cos_ref[...]
    sinT = sin_ref[...]

    @pl.when(blk == 0)
    def _():
        qx_ref[...] = jnp.zeros(qx_ref.shape, BF16)

    qiT = qi_ref[0].astype(F32).T
    qix_ref[...] = jnp.concatenate(
        [_head_T(qiT, h, cosT, sinT, None).astype(BF16) for h in range(IDX_HEADS)], axis=1)
    wT = wi_ref[0].T

    qidx = blk * BLOCK + lax.broadcasted_iota(jnp.int32, (CHUNK, BLOCK), 1)
    krow = lax.broadcasted_iota(jnp.int32, (CHUNK, BLOCK), 0)

    def idx_body(i, carry):
        offs = [pl.multiple_of(jnp.minimum(2 * i + j, nch - 1) * CHUNK, CHUNK) for j in range(2)]
        lgs = [_dot(ki_ref[0, pl.ds(off, CHUNK), :], qix_ref[...]) for off in offs]
        mx, mn = carry
        for off, lg in zip(offs, lgs):
            acc = jnp.zeros((CHUNK, BLOCK), F32)
            for h in range(IDX_HEADS):
                w_h = wT[HEAD_DIM + h:HEAD_DIM + h + 1, :]
                acc = acc + w_h * jnp.maximum(lg[:, BLOCK * h:BLOCK * (h + 1)], 0.0)
            causal = krow + off <= qidx
            hi_part = jnp.where(causal, acc, -jnp.inf)
            lo_part = jnp.where(causal, acc, jnp.inf)
            sc_ref[pl.ds(off, CHUNK), :] = hi_part
            mx = jnp.maximum(mx, jnp.max(hi_part.reshape(CHUNK // FOLD, FOLD, BLOCK), axis=0))
            mn = jnp.minimum(mn, jnp.min(lo_part.reshape(CHUNK // FOLD, FOLD, BLOCK), axis=0))
        return mx, mn

    mx, mn = lax.fori_loop(0, (nch + 1) // 2, idx_body, (jnp.full((FOLD, BLOCK), -jnp.inf, F32),
                                                         jnp.full((FOLD, BLOCK), jnp.inf, F32)))

    @pl.when(nch % 2 == 1)
    def _():
        sc_ref[pl.ds(pl.multiple_of(nch * CHUNK, CHUNK), CHUNK), :] = jnp.full((CHUNK, BLOCK), -jnp.inf, F32)

    thr_ref[...] = jnp.full((1, BLOCK), -jnp.inf, F32)
    cut_ref[...] = jnp.full((1, BLOCK), -1, jnp.int32)

    npass = (nch + 1) // 2
    prow = lax.broadcasted_iota(jnp.int32, (PASS_ROWS, BLOCK), 0)

    def sweep(fn, inits, combine, finish, src=sc_ref):
        def body(c, accs):
            off = pl.multiple_of(c * PASS_ROWS, PASS_ROWS)
            vals = fn(src[pl.ds(off, PASS_ROWS), :], off)
            return tuple(combine(a, v.reshape(PASS_ROWS // FOLD, FOLD, BLOCK)) for a, v in zip(accs, vals))
        return tuple(finish(a) for a in lax.fori_loop(0, npass, body, inits))

    def counts(*preds, src=sc_ref):
        return sweep(lambda s, off: tuple(jnp.where(p(s, off), 1.0, 0.0) for p in preds),
                     (jnp.zeros((FOLD, BLOCK), F32),) * len(preds),
                     lambda a, v: a + jnp.sum(v, axis=0),
                     lambda a: jnp.sum(a, axis=0, keepdims=True), src)

    def min_above(t):
        return sweep(lambda s, off: (jnp.where(s > t, s, jnp.inf),),
                     (jnp.full((FOLD, BLOCK), jnp.inf, F32),),
                     lambda a, v: jnp.minimum(a, jnp.min(v, axis=0)),
                     lambda a: jnp.min(a, axis=0, keepdims=True))[0]

    kf = float(TOPK)

    @pl.when(blk >= 2)
    def _():
        vmax = jnp.max(mx, axis=0, keepdims=True)
        vmin = jnp.min(mn, axis=0, keepdims=True)
        below = vmin - (jnp.abs(vmin) * 1e-3 + 1e-30)

        def bis(_, lohi):
            lo, hi = lohi
            mid = 0.5 * (lo + hi)
            ok = counts(lambda s, off: s > mid)[0] >= kf
            return jnp.where(ok, mid, lo), jnp.where(ok, hi, mid)

        lo, hi = lax.fori_loop(0, N_BISECT, bis, (below, vmax))

        def probe(lo):
            cand = min_above(lo)
            return (cand,) + counts(lambda s, off: s > cand, lambda s, off: s >= cand)

        def walk_cond(st):
            return jnp.max(jnp.where(st[2] >= kf, 1.0, 0.0)) > 0.0

        def walk_body(st):
            lo, cand, gt, _ = st
            lo = jnp.where(gt >= kf, cand, lo)
            return (lo,) + probe(lo)

        _, thr, cnt_gt, cnt_ge = lax.while_loop(walk_cond, walk_body, (lo,) + probe(lo))
        thr_ref[...] = thr
        cut_ref[...] = jnp.full((1, BLOCK), 2 ** IDX_BITS, jnp.int32)

        excess = (cnt_ge - kf).astype(jnp.int32)
        most = jnp.max(excess)

        def mark_ties():
            def mark(c, carry):
                off = pl.multiple_of(c * PASS_ROWS, PASS_ROWS)
                tied = sc_ref[pl.ds(off, PASS_ROWS), :] == thr
                tie_ref[pl.ds(off, PASS_ROWS), :] = jnp.where(tied, prow + off, 2 ** IDX_BITS)
                return carry

            lax.fori_loop(0, npass, mark, 0)

        @pl.when((most > 0) & (most <= TIE_WALK_MAX))
        def _():
            mark_ties()

            def drop(_, st):
                cut, rem = st
                top = sweep(lambda e, off: (jnp.where(e <= cut, e, -1),),
                            (jnp.full((FOLD, BLOCK), -1, jnp.int32),),
                            lambda a, v: jnp.maximum(a, jnp.max(v, axis=0)),
                            lambda a: jnp.max(a, axis=0, keepdims=True), tie_ref)[0]
                return jnp.where(rem > 0, top - 1, cut), rem - 1

            every = jnp.full((1, BLOCK), 2 ** IDX_BITS - 1, jnp.int32)
            cut_ref[...] = lax.fori_loop(0, most, drop, (every, excess))[0]

        @pl.when(most > TIE_WALK_MAX)
        def _():
            need = kf - cnt_gt
            mark_ties()

            grow = jnp.where(npass > 1, 1, 0) + jnp.where(npass > 2, 1, 0) + jnp.where(npass > 4, 1, 0)
            half0 = lax.shift_left(jnp.int32(PASS_ROWS // 2), grow)

            def tie_bis(i, lo_i):
                mid = lo_i + lax.shift_right_logical(half0, i)
                got = counts(lambda e, off: e <= mid, src=tie_ref)[0]
                return jnp.where(got >= need, lo_i, mid)

            nsteps = (PASS_ROWS.bit_length() - 1) + grow
            lo_i = lax.fori_loop(0, nsteps, tie_bis, jnp.full((1, BLOCK), -1, jnp.int32))
            cut_ref[...] = lo_i + 1

    _init_softmax(m_ref, acc_ref)

    def chunk_off(c):
        return pl.multiple_of(jnp.minimum(c, nch - 1) * CHUNK, CHUNK)

    def set_bias(c):
        off = chunk_off(c)
        s_idx = sc_ref[pl.ds(off, CHUNK), :]
        thr = thr_ref[...]
        sel = (s_idx > thr) | ((s_idx == thr) & (krow + off <= cut_ref[...]))
        bias_ref[...] = jnp.where(sel, 0.0, NEG)

    def scores(c, g, s_dst, cm_dst):
        sT = _dot(k_ref[0, pl.ds(chunk_off(c), CHUNK), :], qx_ref[g])
        bias = bias_ref[...]
        s = sT + jnp.concatenate([bias] * A_REP, axis=1)
        cols = slice(A_REP * BLOCK * g, A_REP * BLOCK * (g + 1))
        s_dst[:, cols] = s
        cm_dst[:, cols] = jnp.max(s, axis=0, keepdims=True)

    def probs(g, s_src):
        cols = slice(A_REP * BLOCK * g, A_REP * BLOCK * (g + 1))
        p_ref[:, cols] = jnp.exp2(s_src[:, cols] - m_ref[:, cols]).astype(BF16)

    def values(c, g):
        w = A_REP * BLOCK
        pv = _dot(vt_ref[0, c, VROWS * g:VROWS * (g + 1), :], p_ref[:, w * g:w * (g + 1)])
        for j in range(A_REP):
            h = A_REP * g + j
            acc_ref[h] = al_ref[:, BLOCK * h:BLOCK * (h + 1)] * acc_ref[h] + pv[:, BLOCK * j:BLOCK * (j + 1)]

    def step(c, src, dst):
        _update_max(m_ref, src[1], al_ref)
        if dst is not None:
            set_bias(c + 1)
        for g in range(A_KV_HEADS):
            if dst is not None:
                scores(c + 1, g, *dst)
            probs(g, src[0])
        for g in range(A_KV_HEADS):
            values(c, g)

    buf_a = (s_ref.at[0], cm_ref.at[0])
    buf_b = (s_ref.at[1], cm_ref.at[1])
    set_bias(0)
    qT = q_ref[0].astype(F32).T
    gain = _query_gain(qg_ref)
    for g in range(A_KV_HEADS):
        heads = [_head_T(qT, A_REP * g + j, cosT, sinT, gain).astype(BF16) for j in range(A_REP)]
        qx_ref[g, HEAD_DIM * g:HEAD_DIM * (g + 1), :] = jnp.concatenate(heads, axis=1)
        scores(0, g, *buf_a)

    def pair_body(i, carry):
        step(2 * i, buf_a, buf_b)
        step(2 * i + 1, buf_b, buf_a)
        return carry

    lax.fori_loop(0, nch // 2, pair_body, 0)

    @pl.when(nch % 2 == 1)
    def _():
        step(nch - 1, buf_a, None)

    _finish_heads(A_Q_HEADS, acc_ref, o_ref)


def _dsa_layout_kernel(k_ref, v_ref, ki_ref, k_o, vt_o, ki_o):
    tp = k_ref.shape[1]
    for c in range(k_o.shape[1] // CHUNK):
        lo, hi = c * CHUNK, min((c + 1) * CHUNK, tp)

        def chunk_of(ref):
            x = ref[0, lo:hi, :]
            if hi - lo == CHUNK:
                return x
            return jnp.concatenate([x, jnp.zeros((CHUNK - (hi - lo), x.shape[1]), x.dtype)], axis=0)

        k_o[0, c * CHUNK:(c + 1) * CHUNK, :] = chunk_of(k_ref)
        ki_o[0, c * CHUNK:(c + 1) * CHUNK, :] = chunk_of(ki_ref)
        vt_o[0, c] = _values_T(chunk_of(v_ref).astype(F32), A_KV_HEADS).astype(BF16)


def _dsa_layout(k, v, ki, tk):
    b, tp, _ = k.shape
    nch = tk // CHUNK
    seq = lambda bi: (bi, 0, 0)
    return pl.pallas_call(
        _dsa_layout_kernel,
        grid=(b,),
        in_specs=[
            pl.BlockSpec((1, tp, 256), seq),
            pl.BlockSpec((1, tp, 256), seq),
            pl.BlockSpec((1, tp, HEAD_DIM), seq),
        ],
        out_specs=[
            pl.BlockSpec((1, tk, 256), seq),
            pl.BlockSpec((1, nch, A_KV_HEADS * VROWS, CHUNK), lambda bi: (bi, 0, 0, 0)),
            pl.BlockSpec((1, tk, HEAD_DIM), seq),
        ],
        out_shape=[
            jax.ShapeDtypeStruct((b, tk, 256), BF16),
            jax.ShapeDtypeStruct((b, nch, A_KV_HEADS * VROWS, CHUNK), BF16),
            jax.ShapeDtypeStruct((b, tk, HEAD_DIM), BF16),
        ],
        compiler_params=_cparams("parallel"),
        name="dsa_layout",
    )(k, v, ki)


def _dsa_attention(q, k, vt, qi, ki, wi, cosT, sinT, qgain):
    b, tp, _ = q.shape
    tk = k.shape[1]
    nblk = tp // BLOCK
    qblk = lambda bi, j: (bi, j, 0)
    seq = lambda bi, j: (bi, 0, 0)
    return pl.pallas_call(
        _dsa_kernel,
        grid=(b, nblk),
        in_specs=[
            pl.BlockSpec((1, BLOCK, 1024), qblk),
            pl.BlockSpec((1, tk, 256), seq),
            pl.BlockSpec((1,) + vt.shape[1:], lambda bi, j: (bi, 0, 0, 0)),
            pl.BlockSpec((1, BLOCK, 512), qblk),
            pl.BlockSpec((1, tk, HEAD_DIM), seq),
            pl.BlockSpec((1, BLOCK, LANES), qblk),
            pl.BlockSpec((HALF, BLOCK), lambda bi, j: (0, j)),
            pl.BlockSpec((HALF, BLOCK), lambda bi, j: (0, j)),
            pl.BlockSpec((HEAD_DIM, 1), lambda bi, j: (0, 0)),
        ],
        out_specs=pl.BlockSpec((1, BLOCK, 1024), qblk),
        out_shape=jax.ShapeDtypeStruct((b, tp, 1024), BF16),
        scratch_shapes=[
            pltpu.VMEM((-(-tk // PASS_ROWS) * PASS_ROWS, BLOCK), F32),
            pltpu.VMEM((-(-tk // PASS_ROWS) * PASS_ROWS, BLOCK), jnp.int32),
            pltpu.VMEM((A_KV_HEADS, 256, A_REP * BLOCK), BF16),
            pltpu.VMEM((HEAD_DIM, IDX_HEADS * BLOCK), BF16),
            pltpu.VMEM((1, BLOCK), F32),
            pltpu.VMEM((1, BLOCK), jnp.int32),
            pltpu.VMEM((CHUNK, BLOCK), F32),
            pltpu.VMEM((2, CHUNK, A_Q_HEADS * BLOCK), F32),
            pltpu.VMEM((CHUNK, A_Q_HEADS * BLOCK), BF16),
            pltpu.VMEM((2, 1, A_Q_HEADS * BLOCK), F32),
            pltpu.VMEM((1, A_Q_HEADS * BLOCK), F32),
            pltpu.VMEM((1, A_Q_HEADS * BLOCK), F32),
            pltpu.VMEM((A_Q_HEADS, VROWS, BLOCK), F32),
        ],
        compiler_params=_cparams("parallel", "arbitrary"),
        name="dsa_attn",
    )(q, k, vt, qi, ki, wi, cosT, sinT, qgain)


def _mlp_kernel(h_ref, o_ref, wo_ref, g_ref, w1_ref, w2_ref, out_ref):
    h1 = h_ref[...] + _dot(o_ref[...], wo_ref[...])
    hn = _rms_rows(h1, g_ref[...]).astype(BF16)
    acc = h1
    for c in range(D_FF // FF_CHUNK):
        u = jnp.maximum(_dot(hn, w1_ref[:, FF_CHUNK * c:FF_CHUNK * (c + 1)]), 0.0)
        acc = acc + _dot((u * u).astype(BF16), w2_ref[FF_CHUNK * c:FF_CHUNK * (c + 1), :])
    out_ref[...] = acc


def _out_mlp(h2, o2, wo, g, w1, w2, tm):
    n, d = h2.shape
    const = lambda i: (0, 0)
    tile = lambda i: (i, 0)
    once = pl.Buffered(1)
    return pl.pallas_call(
        _mlp_kernel,
        grid=(n // tm,),
        in_specs=[
            pl.BlockSpec((tm, d), tile),
            pl.BlockSpec((tm, d), tile),
            pl.BlockSpec((d, d), const, pipeline_mode=once),
            pl.BlockSpec((1, d), const),
            pl.BlockSpec((d, D_FF), const, pipeline_mode=once),
            pl.BlockSpec((D_FF, d), const, pipeline_mode=once),
        ],
        out_specs=pl.BlockSpec((tm, d), tile),
        out_shape=jax.ShapeDtypeStruct((n, d), F32),
        compiler_params=_cparams("parallel"),
        name="out_mlp",
    )(h2, o2, wo, g, w1, w2)


def _shared_kv_kernel(h_ref, g_ref, w_ref, e_ref, et_ref, kg_ref, fb_ref, k_o, v_o, lf_o):
    hn = _rms_rows(h_ref[0], g_ref[...]).astype(BF16)
    y = _dot(hn, w_ref[...])
    k_o[0] = _head_norm_nat(y[:, 0:1024], e_ref, et_ref, kg_ref[...]).astype(BF16)
    v_o[0] = y[:, 1024:2048].astype(BF16)
    x = y[:, 2048:2176] + fb_ref[...]
    lf_o[0] = jnp.minimum(x, 0.0) - jnp.log1p(jnp.exp(-jnp.abs(x)))


def _shared_kv(h, g, w_all, e, et, kgain, fbias, tm):
    b, tp, d = h.shape
    const = lambda *_: (0, 0)
    tile = lambda bi, j: (bi, j, 0)
    return pl.pallas_call(
        _shared_kv_kernel,
        grid=(b, tp // tm),
        in_specs=[
            pl.BlockSpec((1, tm, d), tile),
            pl.BlockSpec((1, d), const),
            pl.BlockSpec(w_all.shape, const),
            pl.BlockSpec(e.shape, const),
            pl.BlockSpec(et.shape, const),
            pl.BlockSpec((1, 1024), const),
            pl.BlockSpec((1, LANES), const),
        ],
        out_specs=[
            pl.BlockSpec((1, tm, 1024), tile),
            pl.BlockSpec((1, tm, 1024), tile),
            pl.BlockSpec((1, tm, LANES), tile),
        ],
        out_shape=[
            jax.ShapeDtypeStruct((b, tp, 1024), BF16),
            jax.ShapeDtypeStruct((b, tp, 1024), BF16),
            jax.ShapeDtypeStruct((b, tp, LANES), F32),
        ],
        compiler_params=_cparams("parallel", "parallel"),
        name="shared_kv",
    )(h, g, w_all, e, et, kgain, fbias)


def _proj_q_kernel(h_ref, g_ref, w_ref, q_o):
    hn = _rms_rows(h_ref[...], g_ref[...]).astype(BF16)
    q_o[...] = _dot(hn, w_ref[...]).astype(BF16)


def _proj_q(h2, g, w, tm):
    n, d = h2.shape
    const = lambda i: (0, 0)
    tile = lambda i: (i, 0)
    return pl.pallas_call(
        _proj_q_kernel,
        grid=(n // tm,),
        in_specs=[pl.BlockSpec((tm, d), tile), pl.BlockSpec((1, d), const), pl.BlockSpec((d, d), const)],
        out_specs=pl.BlockSpec((tm, d), tile),
        out_shape=jax.ShapeDtypeStruct((n, d), BF16),
        compiler_params=_cparams("parallel"),
        name="proj_q",
    )(h2, g, w)


def _fox_layout_kernel(tp, k_ref, v_ref, lf_ref, pc_ref, ka_o, vt_o, carry_ref):
    ci = pl.program_id(1)

    @pl.when(ci == 0)
    def _():
        carry_ref[...] = jnp.zeros(carry_ref.shape, F32)

    lf = jnp.where(_valid_rows(ci, tp, lf_ref.shape[1:]), lf_ref[0], 0.0)
    r = lax.broadcasted_iota(jnp.int32, (CHUNK, CHUNK), 0)
    cidx = lax.broadcasted_iota(jnp.int32, (CHUNK, CHUNK), 1)
    tri = jnp.where(cidx <= r, 1.0, 0.0).astype(BF16)
    hi, mid, lo = _split3(lf)
    c = _dot(tri, hi) + _dot(tri, mid) + _dot(tri, lo) + carry_ref[...]
    carry_ref[...] = c[CHUNK - 1:CHUNK, :]

    hi, mid, lo = _split3(c * LOG2E)
    lane = lax.broadcasted_iota(jnp.int32, (CHUNK, LANES), 1)
    packed = jnp.where(lane < B_HEADS, hi.astype(F32),
                       jnp.where(lane < 2 * B_HEADS, pltpu.roll(mid.astype(F32), B_HEADS, 1),
                                 pltpu.roll(lo.astype(F32), 2 * B_HEADS, 1)))
    pieces = _dot(packed.astype(BF16), pc_ref[...])
    k = jnp.where(_valid_rows(ci, tp, k_ref.shape[1:]), k_ref[0], 0)
    first = lax.broadcasted_iota(jnp.int32, (CHUNK, LANES), 1) < HEAD_DIM
    cols = []
    for i in range(B_HEADS // 2):
        kc = k[:, LANES * i:LANES * (i + 1)].astype(F32)
        cols += [jnp.where(first, kc, 0.0), jnp.where(first, pltpu.roll(kc, HEAD_DIM, 1), 0.0)]
    ka_o[0] = (jnp.concatenate(cols, axis=1) + pieces).astype(BF16)
    v = jnp.where(_valid_rows(ci, tp, v_ref.shape[1:]), v_ref[0].astype(F32), 0.0)
    vt_o[0, 0] = _values_T(v, B_HEADS).astype(BF16)


def _fox_layout(k, v, lf, pc, tk):
    b, tp, _ = k.shape
    nch = tk // CHUNK
    chunk = lambda bi, ci: (bi, ci, 0)
    return pl.pallas_call(
        functools.partial(_fox_layout_kernel, tp),
        grid=(b, nch),
        in_specs=[
            pl.BlockSpec((1, CHUNK, 1024), chunk),
            pl.BlockSpec((1, CHUNK, 1024), chunk),
            pl.BlockSpec((1, CHUNK, LANES), chunk),
            pl.BlockSpec(pc.shape, lambda bi, ci: (0, 0)),
        ],
        out_specs=[
            pl.BlockSpec((1, CHUNK, B_HEADS * LANES), chunk),
            pl.BlockSpec((1, 1, B_HEADS * VROWS, CHUNK), lambda bi, ci: (bi, ci, 0, 0)),
        ],
        out_shape=[
            jax.ShapeDtypeStruct((b, tk, B_HEADS * LANES), BF16),
            jax.ShapeDtypeStruct((b, nch, B_HEADS * VROWS, CHUNK), BF16),
        ],
        scratch_shapes=[pltpu.VMEM((1, LANES), F32)],
        compiler_params=_cparams("parallel", "arbitrary"),
        name="fox_layout",
    )(k, v, lf, pc)


def _fox_kernel(q_ref, ka_ref, vt_ref, qg_ref, o_ref, qx_ref, s_ref, p_ref, cm_ref, m_ref, al_ref, acc_ref):
    blk = pl.program_id(1)
    nch = (blk + 2) // 2
    qT = q_ref[0].astype(F32).T
    gain = _query_gain(qg_ref)
    row = lax.broadcasted_iota(jnp.int32, (HEAD_DIM, BLOCK), 0)
    @pl.when(blk == 0)
    def _():
        minus3 = jnp.where(row < 3, -1.0, 0.0).astype(BF16)
        zeros = jnp.zeros((LANES, BLOCK), BF16)
        left = jnp.concatenate([zeros[:HEAD_DIM], minus3, zeros], axis=0)
        right = jnp.concatenate([zeros, zeros[:HEAD_DIM], minus3], axis=0)
        for pr in range(B_HEADS // 2):
            qx_ref[pr] = jnp.concatenate([left, right], axis=1)

    def prep_queries(pr):
        for j in range(2):
            x = _head_T(qT, 2 * pr + j, None, None, gain).astype(BF16)
            qx_ref[pr, 2 * HEAD_DIM * j:2 * HEAD_DIM * j + HEAD_DIM, BLOCK * j:BLOCK * (j + 1)] = x

    _init_softmax(m_ref, acc_ref)
    qidx = blk * BLOCK + lax.broadcasted_iota(jnp.int32, (CHUNK, BLOCK), 1)
    krow = lax.broadcasted_iota(jnp.int32, (CHUNK, BLOCK), 0)

    npair = B_HEADS // 2

    def scores(c, pr, s_dst, cm_dst):
        off = pl.multiple_of(jnp.minimum(c, nch - 1) * CHUNK, CHUNK)
        bias = jnp.where(krow + off <= qidx, 0.0, NEG)
        sT = _dot(ka_ref[0, pl.ds(off, CHUNK), 2 * LANES * pr:2 * LANES * (pr + 1)], qx_ref[pr])
        s = sT + jnp.concatenate([bias, bias], axis=1)
        cols = slice(2 * BLOCK * pr, 2 * BLOCK * (pr + 1))
        s_dst[:, cols] = s
        cm_dst[:, cols] = jnp.max(s, axis=0, keepdims=True)

    def probs(pr, s_src):
        cols = slice(2 * BLOCK * pr, 2 * BLOCK * (pr + 1))
        p_ref[:, cols] = jnp.exp2(s_src[:, cols] - m_ref[:, cols]).astype(BF16)

    def values(c, pr):
        pv = _dot(vt_ref[0, c, 2 * VROWS * pr:2 * VROWS * (pr + 1), :],
                  p_ref[:, 2 * BLOCK * pr:2 * BLOCK * (pr + 1)])
        for j in range(2):
            h = 2 * pr + j
            acc_ref[h] = (al_ref[:, BLOCK * h:BLOCK * (h + 1)] * acc_ref[h]
                          + pv[VROWS * j:VROWS * (j + 1), BLOCK * j:BLOCK * (j + 1)])

    def step(c, src, dst):
        _update_max(m_ref, src[1], al_ref)
        for pr in range(npair):
            if dst is not None:
                scores(c + 1, pr, *dst)
            probs(pr, src[0])
            if pr > 0:
                values(c, pr - 1)
        values(c, npair - 1)

    buf_a = (s_ref.at[0], cm_ref.at[0])
    buf_b = (s_ref.at[1], cm_ref.at[1])
    for pr in range(npair):
        prep_queries(pr)
    for pr in range(npair):
        scores(0, pr, *buf_a)

    def pair_body(i, carry):
        step(2 * i, buf_a, buf_b)
        step(2 * i + 1, buf_b, buf_a)
        return carry

    lax.fori_loop(0, nch // 2, pair_body, 0)

    @pl.when(nch % 2 == 1)
    def _():
        step(nch - 1, buf_a, None)

    _finish_heads(B_HEADS, acc_ref, o_ref)


def _fox_attention(q, ka, vt, qgain):
    b, tp, _ = q.shape
    tk = ka.shape[1]
    qblk = lambda bi, j: (bi, j, 0)
    return pl.pallas_call(
        _fox_kernel,
        grid=(b, tp // BLOCK),
        in_specs=[
            pl.BlockSpec((1, BLOCK, 1024), qblk),
            pl.BlockSpec((1, tk, B_HEADS * LANES), lambda bi, j: (bi, 0, 0)),
            pl.BlockSpec((1,) + vt.shape[1:], lambda bi, j: (bi, 0, 0, 0)),
            pl.BlockSpec((HEAD_DIM, 1), lambda bi, j: (0, 0)),
        ],
        out_specs=pl.BlockSpec((1, BLOCK, 1024), qblk),
        out_shape=jax.ShapeDtypeStruct((b, tp, 1024), BF16),
        scratch_shapes=[
            pltpu.VMEM((B_HEADS // 2, 2 * LANES, 2 * BLOCK), BF16),
            pltpu.VMEM((2, CHUNK, B_HEADS * BLOCK), F32),
            pltpu.VMEM((CHUNK, B_HEADS * BLOCK), BF16),
            pltpu.VMEM((2, 1, B_HEADS * BLOCK), F32),
            pltpu.VMEM((1, B_HEADS * BLOCK), F32),
            pltpu.VMEM((1, B_HEADS * BLOCK), F32),
            pltpu.VMEM((B_HEADS, VROWS, BLOCK), F32),
        ],
        compiler_params=_cparams("parallel", "arbitrary"),
        name="fox_attn",
    )(q, ka, vt, qgain)


def _indicator(nheads, width):
    e = (jnp.arange(nheads * HEAD_DIM)[:, None] // HEAD_DIM == jnp.arange(width)[None, :])
    return e.astype(BF16), e.T.astype(BF16)


def _pad_cols(w, n):
    return jnp.pad(w, ((0, 0), (0, n - w.shape[1])))


def _placements():
    dst = jnp.arange(B_HEADS * LANES)[None, :]
    src = jnp.arange(LANES)[:, None]
    piece, head = src // B_HEADS, src % B_HEADS
    return ((dst == head * LANES + HEAD_DIM + piece) & (piece < 3)).astype(BF16)


def kernel(x, meta_tokens, attn_norm, mlp_norm, mlp_w1, mlp_w2, a_w_in, a_q_gain, a_k_gain, a_w_out,
           kv_norm, kv_w, kv_f_bias, kv_k_gain, b_w_q, b_q_gain, b_w_out):
    b, s, d = x.shape
    t = s + N_META
    tp = -(-t // BLOCK) * BLOCK
    tk = -(-tp // CHUNK) * CHUNK
    tm = tp // 4
    depth = attn_norm.shape[0]
    n_a = a_w_in.shape[0]
    assert d == D_MODEL and tm % 16 == 0 and min(TOPK, s // 4) == TOPK and tk <= 2 ** IDX_BITS

    meta = jnp.broadcast_to(meta_tokens.astype(x.dtype)[None], (b, N_META, d))
    h = jnp.pad(jnp.concatenate([meta, x], axis=1), ((0, 0), (0, tp - t), (0, 0)))

    inv = 1.0 / (ROPE_THETA ** (jnp.arange(0, HEAD_DIM, 2, dtype=F32) / HEAD_DIM))
    ang = jnp.arange(tp, dtype=F32)[:, None] * inv[None, :]
    cos, sin = jnp.cos(ang), jnp.sin(ang)
    c2 = jnp.concatenate([cos, cos, cos, cos], axis=1)
    s2 = jnp.concatenate([-sin, sin, -sin, sin], axis=1)
    cosT, sinT = cos.T, sin.T

    e4, et4 = _indicator(A_KV_HEADS, LANES)
    e16, et16 = _indicator(B_HEADS, LANES)

    shared = None
    for i in range(depth):
        g_attn = attn_norm[i][None, :]
        if i < n_a:
            w_all = _pad_cols(a_w_in[i], 2176).astype(BF16)
            kgain = jnp.tile(a_k_gain[i], A_KV_HEADS)[None, :]
            q, k, v, qi, ki, wi = _proj_a(h, g_attn, w_all, e4, et4, kgain, c2, s2, tm)
            kp, vt, kip = _dsa_layout(k, v, ki, tk)
            o = _dsa_attention(q, kp, vt, qi, kip, wi, cosT, sinT, a_q_gain[i][:, None])
            wo = a_w_out[i]
        else:
            if shared is None:
                w_all = _pad_cols(kv_w, 2176).astype(BF16)
                kgain = jnp.tile(kv_k_gain, B_HEADS)[None, :]
                fbias = jnp.pad(kv_f_bias, (0, LANES - B_HEADS))[None, :]
                k, v, lf = _shared_kv(h, kv_norm[None, :], w_all, e16, et16, kgain, fbias, tm)
                shared = _fox_layout(k, v, lf, _placements(), tk)
            j = i - n_a
            q = _proj_q(h.reshape(b * tp, d), g_attn, b_w_q[j].astype(BF16), tm).reshape(b, tp, d)
            o = _fox_attention(q, shared[0], shared[1], b_q_gain[j][:, None])
            wo = b_w_out[j]
        h = _out_mlp(h.reshape(b * tp, d), o.reshape(b * tp, d), wo.astype(BF16), mlp_norm[i][None, :],
                     mlp_w1[i].astype(BF16), mlp_w2[i].astype(BF16), tm).reshape(b, tp, d)
    return h[:, N_META:N_META + s]
```

```python
import functools
import math

import jax
import jax.numpy as jnp
from jax import lax
from jax.experimental import pallas as pl
from jax.experimental.pallas import tpu as pltpu

F32 = jnp.float32
BF16 = jnp.bfloat16

D_MODEL = 1024
HEAD_DIM = 64
HALF = HEAD_DIM // 2
N_META = 16
BLOCK = 128
CHUNK = 256
PASS_ROWS = 2 * CHUNK
FOLD = 64
A_Q_HEADS = 16
A_KV_HEADS = 4
A_REP = A_Q_HEADS // A_KV_HEADS
IDX_HEADS = 8
B_HEADS = 16
D_FF = 4 * D_MODEL
FF_CHUNK = 1024
TOPK = 256
EPS = 1e-6
ROPE_THETA = 10000.0
LOG2E = math.log2(math.e)
NEG = -1e30
N_BISECT = 17
IDX_BITS = 12
TIE_WALK_MAX = 10
VROWS = 80
LANES = 128
VMEM_LIMIT = 52 * 1024 * 1024


def _cparams(*sem, fuse_inputs=None):
    return pltpu.CompilerParams(dimension_semantics=sem, vmem_limit_bytes=VMEM_LIMIT,
                                allow_input_fusion=fuse_inputs)


def _dot(a, b):
    return jnp.dot(a, b, preferred_element_type=F32)


def _split2(x):
    hi = x.astype(BF16)
    lo = (x - hi.astype(F32)).astype(BF16)
    return hi, lo


def _split3(x):
    hi = x.astype(BF16)
    r = x - hi.astype(F32)
    mid = r.astype(BF16)
    lo = (r - mid.astype(F32)).astype(BF16)
    return hi, mid, lo


def _rms_rows(h, g):
    ms = jnp.mean(h * h, axis=-1, keepdims=True)
    return h * lax.rsqrt(ms + EPS) * g


def _head_norm_nat(x, e_ref, et_ref, gain):
    x2 = x * x
    hi, lo = _split2(x2)
    ss = _dot(hi, e_ref[...]) + _dot(lo, e_ref[...])
    inv = lax.rsqrt(ss * (1.0 / HEAD_DIM) + EPS)
    ihi, ilo = _split2(inv)
    invx = _dot(ihi, et_ref[...]) + _dot(ilo, et_ref[...])
    return x * invx * gain


def _rope_nat(x, c2, s2):
    w = x.shape[1]
    reps = w // LANES
    c = c2 if reps == 1 else jnp.concatenate([c2] * reps, axis=1)
    s = s2 if reps == 1 else jnp.concatenate([s2] * reps, axis=1)
    lane = lax.broadcasted_iota(jnp.int32, x.shape, 1)
    first = (lane & HALF) == 0
    partner = jnp.where(first, pltpu.roll(x, w - HALF, 1), pltpu.roll(x, HALF, 1))
    return x * c + partner * s


def _proj_a_kernel(h_ref, g_ref, w_ref, e_ref, et_ref, kg_ref, c2_ref, s2_ref,
                   q_o, k_o, v_o, qi_o, ki_o, wi_o):
    hn = _rms_rows(h_ref[0], g_ref[...]).astype(BF16)
    y = _dot(hn, w_ref[...])
    q_o[0] = y[:, 0:1024].astype(BF16)
    c2 = c2_ref[...]
    s2 = s2_ref[...]
    k = _head_norm_nat(y[:, 1024:1280], e_ref, et_ref, kg_ref[...])
    k_o[0] = _rope_nat(k, c2, s2).astype(BF16)
    v_o[0] = y[:, 1280:1536].astype(BF16)
    qi_o[0] = y[:, 1536:2048].astype(BF16)
    kiw = y[:, 2048:2176]
    ki_o[0] = _rope_nat(kiw, c2, s2)[:, 0:HEAD_DIM].astype(BF16)
    wi_o[0] = kiw * (IDX_HEADS ** -0.5 * HEAD_DIM ** -0.5)


def _proj_a(h, g, w_all, e, et, kgain, c2, s2, tm):
    b, tp, d = h.shape
    nt = tp // tm
    wn = w_all.shape[1]
    const = lambda *_: (0, 0)
    tile = lambda bi, j: (bi, j, 0)
    return pl.pallas_call(
        _proj_a_kernel,
        grid=(b, nt),
        in_specs=[
            pl.BlockSpec((1, tm, d), tile),
            pl.BlockSpec((1, d), const),
            pl.BlockSpec((d, wn), const),
            pl.BlockSpec(e.shape, const),
            pl.BlockSpec(et.shape, const),
            pl.BlockSpec((1, 256), const),
            pl.BlockSpec((tm, LANES), lambda bi, j: (j, 0)),
            pl.BlockSpec((tm, LANES), lambda bi, j: (j, 0)),
        ],
        out_specs=[
            pl.BlockSpec((1, tm, 1024), tile),
            pl.BlockSpec((1, tm, 256), tile),
            pl.BlockSpec((1, tm, 256), tile),
            pl.BlockSpec((1, tm, 512), tile),
            pl.BlockSpec((1, tm, HEAD_DIM), tile),
            pl.BlockSpec((1, tm, LANES), tile),
        ],
        out_shape=[
            jax.ShapeDtypeStruct((b, tp, 1024), BF16),
            jax.ShapeDtypeStruct((b, tp, 256), BF16),
            jax.ShapeDtypeStruct((b, tp, 256), BF16),
            jax.ShapeDtypeStruct((b, tp, 512), BF16),
            jax.ShapeDtypeStruct((b, tp, HEAD_DIM), BF16),
            jax.ShapeDtypeStruct((b, tp, LANES), F32),
        ],
        compiler_params=_cparams("parallel", "parallel"),
        name="proj_a",
    )(h, g, w_all, e, et, kgain, c2, s2)


def _head_T(xT, h, cosT, sinT, gain):
    x = xT[HEAD_DIM * h:HEAD_DIM * (h + 1)]
    if gain is not None:
        ss = jnp.sum(x * x, axis=0, keepdims=True)
        x = x * lax.rsqrt(ss * (1.0 / HEAD_DIM) + EPS) * gain
    if cosT is not None:
        x1 = x[:HALF]
        x2 = x[HALF:]
        x = jnp.concatenate([x1 * cosT - x2 * sinT, x1 * sinT + x2 * cosT], axis=0)
    return x


def _query_gain(qg_ref):
    return jnp.broadcast_to(qg_ref[...] * (HEAD_DIM ** -0.5 * LOG2E), (HEAD_DIM, BLOCK))


def _update_max(m_ref, cm_ref, al_ref):
    m_old = m_ref[...]
    m_new = jnp.maximum(m_old, cm_ref[...])
    al_ref[...] = jnp.exp2(m_old - m_new)
    m_ref[...] = m_new


def _init_softmax(m_ref, acc_ref):
    m_ref[...] = jnp.full(m_ref.shape, NEG, F32)
    acc_ref[...] = jnp.zeros(acc_ref.shape, F32)


def _finish_heads(nheads, acc_ref, o_ref):
    outs = [acc_ref[h, 0:HEAD_DIM, :] / acc_ref[h, HEAD_DIM:HEAD_DIM + 1, :] for h in range(nheads)]
    oT = jnp.concatenate(outs, axis=0)
    o_ref[0] = oT.T.astype(BF16)


def _values_T(v, nheads):
    vT = v.T
    row = lax.broadcasted_iota(jnp.int32, (VROWS - HEAD_DIM, v.shape[0]), 0)
    ones = jnp.where(row == 0, 1.0, 0.0)
    parts = []
    for h in range(nheads):
        parts += [vT[HEAD_DIM * h:HEAD_DIM * (h + 1)], ones]
    return jnp.concatenate(parts, axis=0)


def _valid_rows(c, tp, shape):
    return c * CHUNK + lax.broadcasted_iota(jnp.int32, shape, 0) < tp


def _dsa_kernel(q_ref, k_ref, vt_ref, qi_ref, ki_ref, wi_ref, cos_ref, sin_ref, qg_ref,
                o_ref, sc_ref, tie_ref, qx_ref, qix_ref, thr_ref, cut_ref, bias_ref, s_ref, p_ref, cm_ref, m_ref,
                al_ref, acc_ref):
    blk = pl.program_id(1)
    nch = (blk + 2) // 2
    cosT = cos_ref[...]
    sinT = sin_ref[...]

    @pl.when(blk == 0)
    def _():
        qx_ref[...] = jnp.zeros(qx_ref.shape, BF16)

    qiT = qi_ref[0].astype(F32).T
    qix_ref[...] = jnp.concatenate(
        [_head_T(qiT, h, cosT, sinT, None).astype(BF16) for h in range(IDX_HEADS)], axis=1)
    wT = wi_ref[0].T

    qidx = blk * BLOCK + lax.broadcasted_iota(jnp.int32, (CHUNK, BLOCK), 1)
    krow = lax.broadcasted_iota(jnp.int32, (CHUNK, BLOCK), 0)

    def idx_body(i, carry):
        offs = [pl.multiple_of(jnp.minimum(2 * i + j, nch - 1) * CHUNK, CHUNK) for j in range(2)]
        lgs = [_dot(ki_ref[0, pl.ds(off, CHUNK), :], qix_ref[...]) for off in offs]
        mx, mn = carry
        for off, lg in zip(offs, lgs):
            acc = jnp.zeros((CHUNK, BLOCK), F32)
            for h in range(IDX_HEADS):
                w_h = wT[HEAD_DIM + h:HEAD_DIM + h + 1, :]
                acc = acc + w_h * jnp.maximum(lg[:, BLOCK * h:BLOCK * (h + 1)], 0.0)
            causal = krow + off <= qidx
            hi_part = jnp.where(causal, acc, -jnp.inf)
            lo_part = jnp.where(causal, acc, jnp.inf)
            sc_ref[pl.ds(off, CHUNK), :] = hi_part
            mx = jnp.maximum(mx, jnp.max(hi_part.reshape(CHUNK // FOLD, FOLD, BLOCK), axis=0))
            mn = jnp.minimum(mn, jnp.min(lo_part.reshape(CHUNK // FOLD, FOLD, BLOCK), axis=0))
        return mx, mn

    mx, mn = lax.fori_loop(0, (nch + 1) // 2, idx_body, (jnp.full((FOLD, BLOCK), -jnp.inf, F32),
                                                         jnp.full((FOLD, BLOCK), jnp.inf, F32)))

    @pl.when(nch % 2 == 1)
    def _():
        sc_ref[pl.ds(pl.multiple_of(nch * CHUNK, CHUNK), CHUNK), :] = jnp.full((CHUNK, BLOCK), -jnp.inf, F32)

    thr_ref[...] = jnp.full((1, BLOCK), -jnp.inf, F32)
    cut_ref[...] = jnp.full((1, BLOCK), -1, jnp.int32)

    npass = (nch + 1) // 2
    prow = lax.broadcasted_iota(jnp.int32, (PASS_ROWS, BLOCK), 0)

    def sweep(fn, inits, combine, finish, src=sc_ref):
        def body(c, accs):
            off = pl.multiple_of(c * PASS_ROWS, PASS_ROWS)
            vals = fn(src[pl.ds(off, PASS_ROWS), :], off)
            return tuple(combine(a, v.reshape(PASS_ROWS // FOLD, FOLD, BLOCK)) for a, v in zip(accs, vals))
        return tuple(finish(a) for a in lax.fori_loop(0, npass, body, inits))

    def counts(*preds, src=sc_ref):
        return sweep(lambda s, off: tuple(jnp.where(p(s, off), 1.0, 0.0) for p in preds),
                     (jnp.zeros((FOLD, BLOCK), F32),) * len(preds),
                     lambda a, v: a + jnp.sum(v, axis=0),
                     lambda a: jnp.sum(a, axis=0, keepdims=True), src)

    def min_above(t):
        return sweep(lambda s, off: (jnp.where(s > t, s, jnp.inf),),
                     (jnp.full((FOLD, BLOCK), jnp.inf, F32),),
                     lambda a, v: jnp.minimum(a, jnp.min(v, axis=0)),
                     lambda a: jnp.min(a, axis=0, keepdims=True))[0]

    kf = float(TOPK)

    @pl.when(blk >= 2)
    def _():
        vmax = jnp.max(mx, axis=0, keepdims=True)
        vmin = jnp.min(mn, axis=0, keepdims=True)
        below = vmin - (jnp.abs(vmin) * 1e-3 + 1e-30)

        def bis(_, lohi):
            lo, hi = lohi
            mid = 0.5 * (lo + hi)
            ok = counts(lambda s, off: s > mid)[0] >= kf
            return jnp.where(ok, mid, lo), jnp.where(ok, hi, mid)

        lo, hi = lax.fori_loop(0, N_BISECT, bis, (below, vmax))

        def probe(lo):
            cand = min_above(lo)
            return (cand,) + counts(lambda s, off: s > cand, lambda s, off: s >= cand)

        def walk_cond(st):
            return jnp.max(jnp.where(st[2] >= kf, 1.0, 0.0)) > 0.0

        def walk_body(st):
            lo, cand, gt, _ = st
            lo = jnp.where(gt >= kf, cand, lo)
            return (lo,) + probe(lo)

        _, thr, cnt_gt, cnt_ge = lax.while_loop(walk_cond, walk_body, (lo,) + probe(lo))
        thr_ref[...] = thr
        cut_ref[...] = jnp.full((1, BLOCK), 2 ** IDX_BITS, jnp.int32)

        excess = (cnt_ge - kf).astype(jnp.int32)
        most = jnp.max(excess)

        def mark_ties():
            def mark(c, carry):
                off = pl.multiple_of(c * PASS_ROWS, PASS_ROWS)
                tied = sc_ref[pl.ds(off, PASS_ROWS), :] == thr
                tie_ref[pl.ds(off, PASS_ROWS), :] = jnp.where(tied, prow + off, 2 ** IDX_BITS)
                return carry

            lax.fori_loop(0, npass, mark, 0)

        @pl.when((most > 0) & (most <= TIE_WALK_MAX))
        def _():
            mark_ties()

            def drop(_, st):
                cut, rem = st
                top = sweep(lambda e, off: (jnp.where(e <= cut, e, -1),),
                            (jnp.full((FOLD, BLOCK), -1, jnp.int32),),
                            lambda a, v: jnp.maximum(a, jnp.max(v, axis=0)),
                            lambda a: jnp.max(a, axis=0, keepdims=True), tie_ref)[0]
                return jnp.where(rem > 0, top - 1, cut), rem - 1

            every = jnp.full((1, BLOCK), 2 ** IDX_BITS - 1, jnp.int32)
            cut_ref[...] = lax.fori_loop(0, most, drop, (every, excess))[0]

        @pl.when(most > TIE_WALK_MAX)
        def _():
            need = kf - cnt_gt
            mark_ties()

            grow = jnp.where(npass > 1, 1, 0) + jnp.where(npass > 2, 1, 0) + jnp.where(npass > 4, 1, 0)
            half0 = lax.shift_left(jnp.int32(PASS_ROWS // 2), grow)

            def tie_bis(i, lo_i):
                mid = lo_i + lax.shift_right_logical(half0, i)
                got = counts(lambda e, off: e <= mid, src=tie_ref)[0]
                return jnp.where(got >= need, lo_i, mid)

            nsteps = (PASS_ROWS.bit_length() - 1) + grow
            lo_i = lax.fori_loop(0, nsteps, tie_bis, jnp.full((1, BLOCK), -1, jnp.int32))
            cut_ref[...] = lo_i + 1

    _init_softmax(m_ref, acc_ref)

    def chunk_off(c):
        return pl.multiple_of(jnp.minimum(c, nch - 1) * CHUNK, CHUNK)

    def set_bias(c):
        off = chunk_off(c)
        s_idx = sc_ref[pl.ds(off, CHUNK), :]
        thr = thr_ref[...]
        sel = (s_idx > thr) | ((s_idx == thr) & (krow + off <= cut_ref[...]))
        bias_ref[...] = jnp.where(sel, 0.0, NEG)

    def scores(c, g, s_dst, cm_dst):
        sT = _dot(k_ref[0, pl.ds(chunk_off(c), CHUNK), :], qx_ref[g])
        bias = bias_ref[...]
        s = sT + jnp.concatenate([bias] * A_REP, axis=1)
        cols = slice(A_REP * BLOCK * g, A_REP * BLOCK * (g + 1))
        s_dst[:, cols] = s
        cm_dst[:, cols] = jnp.max(s, axis=0, keepdims=True)

    def probs(g, s_src):
        cols = slice(A_REP * BLOCK * g, A_REP * BLOCK * (g + 1))
        p_ref[:, cols] = jnp.exp2(s_src[:, cols] - m_ref[:, cols]).astype(BF16)

    def values(c, g):
        w = A_REP * BLOCK
        pv = _dot(vt_ref[0, c, VROWS * g:VROWS * (g + 1), :], p_ref[:, w * g:w * (g + 1)])
        for j in range(A_REP):
            h = A_REP * g + j
            acc_ref[h] = al_ref[:, BLOCK * h:BLOCK * (h + 1)] * acc_ref[h] + pv[:, BLOCK * j:BLOCK * (j + 1)]

    def step(c, src, dst):
        _update_max(m_ref, src[1], al_ref)
        if dst is not None:
            set_bias(c + 1)
        for g in range(A_KV_HEADS):
            if dst is not None:
                scores(c + 1, g, *dst)
            probs(g, src[0])
        for g in range(A_KV_HEADS):
            values(c, g)

    buf_a = (s_ref.at[0], cm_ref.at[0])
    buf_b = (s_ref.at[1], cm_ref.at[1])
    set_bias(0)
    qT = q_ref[0].astype(F32).T
    gain = _query_gain(qg_ref)
    for g in range(A_KV_HEADS):
        heads = [_head_T(qT, A_REP * g + j, cosT, sinT, gain).astype(BF16) for j in range(A_REP)]
        qx_ref[g, HEAD_DIM * g:HEAD_DIM * (g + 1), :] = jnp.concatenate(heads, axis=1)
        scores(0, g, *buf_a)

    def pair_body(i, carry):
        step(2 * i, buf_a, buf_b)
        step(2 * i + 1, buf_b, buf_a)
        return carry

    lax.fori_loop(0, nch // 2, pair_body, 0)

    @pl.when(nch % 2 == 1)
    def _():
        step(nch - 1, buf_a, None)

    _finish_heads(A_Q_HEADS, acc_ref, o_ref)


def _dsa_layout_kernel(k_ref, v_ref, ki_ref, k_o, vt_o, ki_o):
    tp = k_ref.shape[1]
    for c in range(k_o.shape[1] // CHUNK):
        lo, hi = c * CHUNK, min((c + 1) * CHUNK, tp)

        def chunk_of(ref):
            x = ref[0, lo:hi, :]
            if hi - lo == CHUNK:
                return x
            return jnp.concatenate([x, jnp.zeros((CHUNK - (hi - lo), x.shape[1]), x.dtype)], axis=0)

        k_o[0, c * CHUNK:(c + 1) * CHUNK, :] = chunk_of(k_ref)
        ki_o[0, c * CHUNK:(c + 1) * CHUNK, :] = chunk_of(ki_ref)
        vt_o[0, c] = _values_T(chunk_of(v_ref).astype(F32), A_KV_HEADS).astype(BF16)


def _dsa_layout(k, v, ki, tk):
    b, tp, _ = k.shape
    nch = tk // CHUNK
    seq = lambda bi: (bi, 0, 0)
    return pl.pallas_call(
        _dsa_layout_kernel,
        grid=(b,),
        in_specs=[
            pl.BlockSpec((1, tp, 256), seq),
            pl.BlockSpec((1, tp, 256), seq),
            pl.BlockSpec((1, tp, HEAD_DIM), seq),
        ],
        out_specs=[
            pl.BlockSpec((1, tk, 256), seq),
            pl.BlockSpec((1, nch, A_KV_HEADS * VROWS, CHUNK), lambda bi: (bi, 0, 0, 0)),
            pl.BlockSpec((1, tk, HEAD_DIM), seq),
        ],
        out_shape=[
            jax.ShapeDtypeStruct((b, tk, 256), BF16),
            jax.ShapeDtypeStruct((b, nch, A_KV_HEADS * VROWS, CHUNK), BF16),
            jax.ShapeDtypeStruct((b, tk, HEAD_DIM), BF16),
        ],
        compiler_params=_cparams("parallel"),
        name="dsa_layout",
    )(k, v, ki)


def _dsa_attention(q, k, vt, qi, ki, wi, cosT, sinT, qgain):
    b, tp, _ = q.shape
    tk = k.shape[1]
    nblk = tp // BLOCK
    qblk = lambda bi, j: (bi, j, 0)
    seq = lambda bi, j: (bi, 0, 0)
    return pl.pallas_call(
        _dsa_kernel,
        grid=(b, nblk),
        in_specs=[
            pl.BlockSpec((1, BLOCK, 1024), qblk),
            pl.BlockSpec((1, tk, 256), seq),
            pl.BlockSpec((1,) + vt.shape[1:], lambda bi, j: (bi, 0, 0, 0)),
            pl.BlockSpec((1, BLOCK, 512), qblk),
            pl.BlockSpec((1, tk, HEAD_DIM), seq),
            pl.BlockSpec((1, BLOCK, LANES), qblk),
            pl.BlockSpec((HALF, BLOCK), lambda bi, j: (0, j)),
            pl.BlockSpec((HALF, BLOCK), lambda bi, j: (0, j)),
            pl.BlockSpec((HEAD_DIM, 1), lambda bi, j: (0, 0)),
        ],
        out_specs=pl.BlockSpec((1, BLOCK, 1024), qblk),
        out_shape=jax.ShapeDtypeStruct((b, tp, 1024), BF16),
        scratch_shapes=[
            pltpu.VMEM((-(-tk // PASS_ROWS) * PASS_ROWS, BLOCK), F32),
            pltpu.VMEM((-(-tk // PASS_ROWS) * PASS_ROWS, BLOCK), jnp.int32),
            pltpu.VMEM((A_KV_HEADS, 256, A_REP * BLOCK), BF16),
            pltpu.VMEM((HEAD_DIM, IDX_HEADS * BLOCK), BF16),
            pltpu.VMEM((1, BLOCK), F32),
            pltpu.VMEM((1, BLOCK), jnp.int32),
            pltpu.VMEM((CHUNK, BLOCK), F32),
            pltpu.VMEM((2, CHUNK, A_Q_HEADS * BLOCK), F32),
            pltpu.VMEM((CHUNK, A_Q_HEADS * BLOCK), BF16),
            pltpu.VMEM((2, 1, A_Q_HEADS * BLOCK), F32),
            pltpu.VMEM((1, A_Q_HEADS * BLOCK), F32),
            pltpu.VMEM((1, A_Q_HEADS * BLOCK), F32),
            pltpu.VMEM((A_Q_HEADS, VROWS, BLOCK), F32),
        ],
        compiler_params=_cparams("parallel", "arbitrary"),
        name="dsa_attn",
    )(q, k, vt, qi, ki, wi, cosT, sinT, qgain)


def _mlp_kernel(h_ref, o_ref, wo_ref, g_ref, w1_ref, w2_ref, out_ref):
    h1 = h_ref[...] + _dot(o_ref[...], wo_ref[...])
    hn = _rms_rows(h1, g_ref[...]).astype(BF16)
    acc = h1
    for c in range(D_FF // FF_CHUNK):
        u = jnp.maximum(_dot(hn, w1_ref[:, FF_CHUNK * c:FF_CHUNK * (c + 1)]), 0.0)
        acc = acc + _dot((u * u).astype(BF16), w2_ref[FF_CHUNK * c:FF_CHUNK * (c + 1), :])
    out_ref[...] = acc


def _out_mlp(h2, o2, wo, g, w1, w2, tm):
    n, d = h2.shape
    const = lambda i: (0, 0)
    tile = lambda i: (i, 0)
    once = pl.Buffered(1)
    return pl.pallas_call(
        _mlp_kernel,
        grid=(n // tm,),
        in_specs=[
            pl.BlockSpec((tm, d), tile),
            pl.BlockSpec((tm, d), tile),
            pl.BlockSpec((d, d), const, pipeline_mode=once),
            pl.BlockSpec((1, d), const),
            pl.BlockSpec((d, D_FF), const, pipeline_mode=once),
            pl.BlockSpec((D_FF, d), const, pipeline_mode=once),
        ],
        out_specs=pl.BlockSpec((tm, d), tile),
        out_shape=jax.ShapeDtypeStruct((n, d), F32),
        compiler_params=_cparams("parallel", fuse_inputs=[False, False, True, False, True, True]),
        name="out_mlp",
    )(h2, o2, wo, g, w1, w2)


def _shared_kv_kernel(h_ref, g_ref, w_ref, e_ref, et_ref, kg_ref, fb_ref, k_o, v_o, lf_o):
    hn = _rms_rows(h_ref[0], g_ref[...]).astype(BF16)
    y = _dot(hn, w_ref[...])
    k_o[0] = _head_norm_nat(y[:, 0:1024], e_ref, et_ref, kg_ref[...]).astype(BF16)
    v_o[0] = y[:, 1024:2048].astype(BF16)
    x = y[:, 2048:2176] + fb_ref[...]
    lf_o[0] = jnp.minimum(x, 0.0) - jnp.log1p(jnp.exp(-jnp.abs(x)))


def _shared_kv(h, g, w_all, e, et, kgain, fbias, tm):
    b, tp, d = h.shape
    const = lambda *_: (0, 0)
    tile = lambda bi, j: (bi, j, 0)
    return pl.pallas_call(
        _shared_kv_kernel,
        grid=(b, tp // tm),
        in_specs=[
            pl.BlockSpec((1, tm, d), tile),
            pl.BlockSpec((1, d), const),
            pl.BlockSpec(w_all.shape, const),
            pl.BlockSpec(e.shape, const),
            pl.BlockSpec(et.shape, const),
            pl.BlockSpec((1, 1024), const),
            pl.BlockSpec((1, LANES), const),
        ],
        out_specs=[
            pl.BlockSpec((1, tm, 1024), tile),
            pl.BlockSpec((1, tm, 1024), tile),
            pl.BlockSpec((1, tm, LANES), tile),
        ],
        out_shape=[
            jax.ShapeDtypeStruct((b, tp, 1024), BF16),
            jax.ShapeDtypeStruct((b, tp, 1024), BF16),
            jax.ShapeDtypeStruct((b, tp, LANES), F32),
        ],
        compiler_params=_cparams("parallel", "parallel"),
        name="shared_kv",
    )(h, g, w_all, e, et, kgain, fbias)


def _proj_q_kernel(h_ref, g_ref, w_ref, q_o):
    hn = _rms_rows(h_ref[...], g_ref[...]).astype(BF16)
    q_o[...] = _dot(hn, w_ref[...]).astype(BF16)


def _proj_q(h2, g, w, tm):
    n, d = h2.shape
    const = lambda i: (0, 0)
    tile = lambda i: (i, 0)
    return pl.pallas_call(
        _proj_q_kernel,
        grid=(n // tm,),
        in_specs=[pl.BlockSpec((tm, d), tile), pl.BlockSpec((1, d), const), pl.BlockSpec((d, d), const)],
        out_specs=pl.BlockSpec((tm, d), tile),
        out_shape=jax.ShapeDtypeStruct((n, d), BF16),
        compiler_params=_cparams("parallel", fuse_inputs=[False, False, True]),
        name="proj_q",
    )(h2, g, w)


def _fox_layout_kernel(tp, k_ref, v_ref, lf_ref, pc_ref, ka_o, vt_o, carry_ref):
    ci = pl.program_id(1)

    @pl.when(ci == 0)
    def _():
        carry_ref[...] = jnp.zeros(carry_ref.shape, F32)

    lf = jnp.where(_valid_rows(ci, tp, lf_ref.shape[1:]), lf_ref[0], 0.0)
    r = lax.broadcasted_iota(jnp.int32, (CHUNK, CHUNK), 0)
    cidx = lax.broadcasted_iota(jnp.int32, (CHUNK, CHUNK), 1)
    tri = jnp.where(cidx <= r, 1.0, 0.0).astype(BF16)
    hi, mid, lo = _split3(lf)
    c = _dot(tri, hi) + _dot(tri, mid) + _dot(tri, lo) + carry_ref[...]
    carry_ref[...] = c[CHUNK - 1:CHUNK, :]

    hi, mid, lo = _split3(c * LOG2E)
    lane = lax.broadcasted_iota(jnp.int32, (CHUNK, LANES), 1)
    packed = jnp.where(lane < B_HEADS, hi.astype(F32),
                       jnp.where(lane < 2 * B_HEADS, pltpu.roll(mid.astype(F32), B_HEADS, 1),
                                 pltpu.roll(lo.astype(F32), 2 * B_HEADS, 1)))
    pieces = _dot(packed.astype(BF16), pc_ref[...])
    k = jnp.where(_valid_rows(ci, tp, k_ref.shape[1:]), k_ref[0], 0)
    first = lax.broadcasted_iota(jnp.int32, (CHUNK, LANES), 1) < HEAD_DIM
    cols = []
    for i in range(B_HEADS // 2):
        kc = k[:, LANES * i:LANES * (i + 1)].astype(F32)
        cols += [jnp.where(first, kc, 0.0), jnp.where(first, pltpu.roll(kc, HEAD_DIM, 1), 0.0)]
    ka_o[0] = (jnp.concatenate(cols, axis=1) + pieces).astype(BF16)
    v = jnp.where(_valid_rows(ci, tp, v_ref.shape[1:]), v_ref[0].astype(F32), 0.0)
    vt_o[0, 0] = _values_T(v, B_HEADS).astype(BF16)


def _fox_layout(k, v, lf, pc, tk):
    b, tp, _ = k.shape
    nch = tk // CHUNK
    chunk = lambda bi, ci: (bi, ci, 0)
    return pl.pallas_call(
        functools.partial(_fox_layout_kernel, tp),
        grid=(b, nch),
        in_specs=[
            pl.BlockSpec((1, CHUNK, 1024), chunk),
            pl.BlockSpec((1, CHUNK, 1024), chunk),
            pl.BlockSpec((1, CHUNK, LANES), chunk),
            pl.BlockSpec(pc.shape, lambda bi, ci: (0, 0)),
        ],
        out_specs=[
            pl.BlockSpec((1, CHUNK, B_HEADS * LANES), chunk),
            pl.BlockSpec((1, 1, B_HEADS * VROWS, CHUNK), lambda bi, ci: (bi, ci, 0, 0)),
        ],
        out_shape=[
            jax.ShapeDtypeStruct((b, tk, B_HEADS * LANES), BF16),
            jax.ShapeDtypeStruct((b, nch, B_HEADS * VROWS, CHUNK), BF16),
        ],
        scratch_shapes=[pltpu.VMEM((1, LANES), F32)],
        compiler_params=_cparams("parallel", "arbitrary"),
        name="fox_layout",
    )(k, v, lf, pc)


def _fox_kernel(q_ref, ka_ref, vt_ref, qg_ref, o_ref, qx_ref, s_ref, p_ref, cm_ref, m_ref, al_ref, acc_ref):
    blk = pl.program_id(1)
    nch = (blk + 2) // 2
    qT = q_ref[0].astype(F32).T
    gain = _query_gain(qg_ref)
    row = lax.broadcasted_iota(jnp.int32, (HEAD_DIM, BLOCK), 0)
    @pl.when(blk == 0)
    def _():
        minus3 = jnp.where(row < 3, -1.0, 0.0).astype(BF16)
        zeros = jnp.zeros((LANES, BLOCK), BF16)
        left = jnp.concatenate([zeros[:HEAD_DIM], minus3, zeros], axis=0)
        right = jnp.concatenate([zeros, zeros[:HEAD_DIM], minus3], axis=0)
        for pr in range(B_HEADS // 2):
            qx_ref[pr] = jnp.concatenate([left, right], axis=1)

    def prep_queries(pr):
        for j in range(2):
            x = _head_T(qT, 2 * pr + j, None, None, gain).astype(BF16)
            qx_ref[pr, 2 * HEAD_DIM * j:2 * HEAD_DIM * j + HEAD_DIM, BLOCK * j:BLOCK * (j + 1)] = x

    _init_softmax(m_ref, acc_ref)
    qidx = blk * BLOCK + lax.broadcasted_iota(jnp.int32, (CHUNK, BLOCK), 1)
    krow = lax.broadcasted_iota(jnp.int32, (CHUNK, BLOCK), 0)

    npair = B_HEADS // 2

    def scores(c, pr, s_dst, cm_dst):
        off = pl.multiple_of(jnp.minimum(c, nch - 1) * CHUNK, CHUNK)
        bias = jnp.where(krow + off <= qidx, 0.0, NEG)
        sT = _dot(ka_ref[0, pl.ds(off, CHUNK), 2 * LANES * pr:2 * LANES * (pr + 1)], qx_ref[pr])
        s = sT + jnp.concatenate([bias, bias], axis=1)
        cols = slice(2 * BLOCK * pr, 2 * BLOCK * (pr + 1))
        s_dst[:, cols] = s
        cm_dst[:, cols] = jnp.max(s, axis=0, keepdims=True)

    def probs(pr, s_src):
        cols = slice(2 * BLOCK * pr, 2 * BLOCK * (pr + 1))
        p_ref[:, cols] = jnp.exp2(s_src[:, cols] - m_ref[:, cols]).astype(BF16)

    def values(c, pr):
        pv = _dot(vt_ref[0, c, 2 * VROWS * pr:2 * VROWS * (pr + 1), :],
                  p_ref[:, 2 * BLOCK * pr:2 * BLOCK * (pr + 1)])
        for j in range(2):
            h = 2 * pr + j
            acc_ref[h] = (al_ref[:, BLOCK * h:BLOCK * (h + 1)] * acc_ref[h]
                          + pv[VROWS * j:VROWS * (j + 1), BLOCK * j:BLOCK * (j + 1)])

    def step(c, src, dst):
        _update_max(m_ref, src[1], al_ref)
        for pr in range(npair):
            if dst is not None:
                scores(c + 1, pr, *dst)
            probs(pr, src[0])
            if pr > 0:
                values(c, pr - 1)
        values(c, npair - 1)

    buf_a = (s_ref.at[0], cm_ref.at[0])
    buf_b = (s_ref.at[1], cm_ref.at[1])
    for pr in range(npair):
        prep_queries(pr)
    for pr in range(npair):
        scores(0, pr, *buf_a)

    def pair_body(i, carry):
        step(2 * i, buf_a, buf_b)
        step(2 * i + 1, buf_b, buf_a)
        return carry

    lax.fori_loop(0, nch // 2, pair_body, 0)

    @pl.when(nch % 2 == 1)
    def _():
        step(nch - 1, buf_a, None)

    _finish_heads(B_HEADS, acc_ref, o_ref)


def _fox_attention(q, ka, vt, qgain):
    b, tp, _ = q.shape
    tk = ka.shape[1]
    qblk = lambda bi, j: (bi, j, 0)
    return pl.pallas_call(
        _fox_kernel,
        grid=(b, tp // BLOCK),
        in_specs=[
            pl.BlockSpec((1, BLOCK, 1024), qblk),
            pl.BlockSpec((1, tk, B_HEADS * LANES), lambda bi, j: (bi, 0, 0)),
            pl.BlockSpec((1,) + vt.shape[1:], lambda bi, j: (bi, 0, 0, 0)),
            pl.BlockSpec((HEAD_DIM, 1), lambda bi, j: (0, 0)),
        ],
        out_specs=pl.BlockSpec((1, BLOCK, 1024), qblk),
        out_shape=jax.ShapeDtypeStruct((b, tp, 1024), BF16),
        scratch_shapes=[
            pltpu.VMEM((B_HEADS // 2, 2 * LANES, 2 * BLOCK), BF16),
            pltpu.VMEM((2, CHUNK, B_HEADS * BLOCK), F32),
            pltpu.VMEM((CHUNK, B_HEADS * BLOCK), BF16),
            pltpu.VMEM((2, 1, B_HEADS * BLOCK), F32),
            pltpu.VMEM((1, B_HEADS * BLOCK), F32),
            pltpu.VMEM((1, B_HEADS * BLOCK), F32),
            pltpu.VMEM((B_HEADS, VROWS, BLOCK), F32),
        ],
        compiler_params=_cparams("parallel", "arbitrary"),
        name="fox_attn",
    )(q, ka, vt, qgain)


def _indicator(nheads, width):
    e = (jnp.arange(nheads * HEAD_DIM)[:, None] // HEAD_DIM == jnp.arange(width)[None, :])
    return e.astype(BF16), e.T.astype(BF16)


def _pad_cols(w, n):
    return jnp.pad(w, ((0, 0), (0, n - w.shape[1])))


def _placements():
    dst = jnp.arange(B_HEADS * LANES)[None, :]
    src = jnp.arange(LANES)[:, None]
    piece, head = src // B_HEADS, src % B_HEADS
    return ((dst == head * LANES + HEAD_DIM + piece) & (piece < 3)).astype(BF16)


def kernel(x, meta_tokens, attn_norm, mlp_norm, mlp_w1, mlp_w2, a_w_in, a_q_gain, a_k_gain, a_w_out,
           kv_norm, kv_w, kv_f_bias, kv_k_gain, b_w_q, b_q_gain, b_w_out):
    b, s, d = x.shape
    t = s + N_META
    tp = -(-t // BLOCK) * BLOCK
    tk = -(-tp // CHUNK) * CHUNK
    tm = tp // 4
    depth = attn_norm.shape[0]
    n_a = a_w_in.shape[0]
    assert d == D_MODEL and tm % 16 == 0 and min(TOPK, s // 4) == TOPK and tk <= 2 ** IDX_BITS

    meta = jnp.broadcast_to(meta_tokens.astype(x.dtype)[None], (b, N_META, d))
    h = jnp.pad(jnp.concatenate([meta, x], axis=1), ((0, 0), (0, tp - t), (0, 0)))

    inv = 1.0 / (ROPE_THETA ** (jnp.arange(0, HEAD_DIM, 2, dtype=F32) / HEAD_DIM))
    ang = jnp.arange(tp, dtype=F32)[:, None] * inv[None, :]
    cos, sin = jnp.cos(ang), jnp.sin(ang)
    c2 = jnp.concatenate([cos, cos, cos, cos], axis=1)
    s2 = jnp.concatenate([-sin, sin, -sin, sin], axis=1)
    cosT, sinT = cos.T, sin.T

    e4, et4 = _indicator(A_KV_HEADS, LANES)
    e16, et16 = _indicator(B_HEADS, LANES)

    shared = None
    for i in range(depth):
        g_attn = attn_norm[i][None, :]
        if i < n_a:
            w_all = _pad_cols(a_w_in[i], 2176).astype(BF16)
            kgain = jnp.tile(a_k_gain[i], A_KV_HEADS)[None, :]
            q, k, v, qi, ki, wi = _proj_a(h, g_attn, w_all, e4, et4, kgain, c2, s2, tm)
            kp, vt, kip = _dsa_layout(k, v, ki, tk)
            o = _dsa_attention(q, kp, vt, qi, kip, wi, cosT, sinT, a_q_gain[i][:, None])
            wo = a_w_out[i]
        else:
            if shared is None:
                w_all = _pad_cols(kv_w, 2176).astype(BF16)
                kgain = jnp.tile(kv_k_gain, B_HEADS)[None, :]
                fbias = jnp.pad(kv_f_bias, (0, LANES - B_HEADS))[None, :]
                k, v, lf = _shared_kv(h, kv_norm[None, :], w_all, e16, et16, kgain, fbias, tm)
                shared = _fox_layout(k, v, lf, _placements(), tk)
            j = i - n_a
            q = _proj_q(h.reshape(b * tp, d), g_attn, b_w_q[j].astype(BF16), tm).reshape(b, tp, d)
            o = _fox_attention(q, shared[0], shared[1], b_q_gain[j][:, None])
            wo = b_w_out[j]
        h = _out_mlp(h.reshape(b * tp, d), o.reshape(b * tp, d), wo.astype(BF16), mlp_norm[i][None, :],
                     mlp_w1[i].astype(BF16), mlp_w2[i].astype(BF16), tm).reshape(b, tp, d)
    return h[:, N_META:N_META + s]
```
